```python
import math
import jax
import jax.numpy as jnp
from jax import lax
import numpy as np

D_MODEL = 4096
BATCH = 4
SEQ = 2048
DEPTH = 4
DEC_BATCH = 8
DEC_SEQ = 4
PAST_LEN = 8192
PAGE_SIZE = 128

SSM_WIDTH = D_MODEL // 2
SSM_GROUP = 16
SSM_GROUPS = SSM_WIDTH // SSM_GROUP
SSM_STATE = 64
DT_MIN = 1e-3
DT_MAX = 1e-1
CONV_CH = D_MODEL // 2
CONV_K = 3
FOX_HEAD_DIM = 128
FOX_HEADS = D_MODEL // FOX_HEAD_DIM
FOX_WIDTH = FOX_HEADS * FOX_HEAD_DIM
FGATE_BIAS_MEAN = 3.0
Q_BLOCK = 128
MEM_LEN = 256
MEM_HEADS = 4
MEM_HEAD_DIM = 128
MEM_WIDTH = MEM_HEADS * MEM_HEAD_DIM
N_SSM_LAYERS = (DEPTH + 1) // 2
N_FOX_LAYERS = DEPTH // 2
DN_ALPHA = (2 * DEPTH) ** 0.25
DN_BETA = (8 * DEPTH) ** -0.25
LN_EPS = 1e-5
EVEN_SPLITS = (SSM_WIDTH, SSM_WIDTH, CONV_CH, CONV_CH, CONV_CH, CONV_CH, MEM_WIDTH, MEM_WIDTH)
ODD_SPLITS = (FOX_WIDTH, FOX_WIDTH, FOX_WIDTH, FOX_WIDTH, FOX_HEADS, MEM_WIDTH, MEM_WIDTH)
EVEN_COLS = sum(EVEN_SPLITS)
ODD_COLS = sum(ODD_SPLITS)
EVEN_MIX = SSM_WIDTH + CONV_CH + MEM_WIDTH
ODD_MIX = FOX_WIDTH + MEM_WIDTH

kernel_name = "hybrid_s5_conv_fox_memory_decoder_step"

F32 = jnp.float32


def _split(z, sizes):
    idx = np.cumsum(np.array(sizes))[:-1].tolist()
    return jnp.split(z, idx, axis=-1)


def _layer_norm(x, g, b):
    xf = x.astype(F32)
    mu = jnp.mean(xf, axis=-1, keepdims=True)
    var = jnp.mean(jnp.square(xf - mu), axis=-1, keepdims=True)
    y = (xf - mu) * lax.rsqrt(var + LN_EPS) * g.astype(F32) + b.astype(F32)
    return y.astype(x.dtype)


def _mem_attend(q, mem_k, mem_v):
    n, t, _ = q.shape
    qh = q.reshape(n, t, MEM_HEADS, MEM_HEAD_DIM)
    logits = jnp.einsum('bqhd,bkhd->bhqk', qh, mem_k).astype(F32) * (MEM_HEAD_DIM ** -0.5)
    p = jax.nn.softmax(logits, axis=-1).astype(mem_v.dtype)
    return jnp.einsum('bhqk,bkhd->bqhd', p, mem_v).reshape(n, t, MEM_WIDTH)


def _s5_scan(u, h0_re, h0_im, lam_re, lam_im, log_dt, b_re, b_im, c_re, c_im, d_skip):
    n, t, _ = u.shape
    ug = u.astype(F32).reshape(n, t, SSM_GROUPS, SSM_GROUP)
    lam_re = lam_re.astype(F32)
    lam_im = lam_im.astype(F32)
    dt = jnp.exp(log_dt.astype(F32))[:, None]
    mag = jnp.exp(lam_re * dt)
    ar = mag * jnp.cos(lam_im * dt)
    ai = mag * jnp.sin(lam_im * dt)
    nr = ar - 1.0
    den = jnp.square(lam_re) + jnp.square(lam_im)
    cr = ((nr * lam_re + ai * lam_im) / den)[..., None]
    ci = ((ai * lam_re - nr * lam_im) / den)[..., None]
    b_re = b_re.astype(F32)
    b_im = b_im.astype(F32)
    bb_re = cr * b_re - ci * b_im
    bb_im = cr * b_im + ci * b_re
    bu_re = jnp.einsum('btgh,gph->btgp', ug, bb_re)
    bu_im = jnp.einsum('btgh,gph->btgp', ug, bb_im)
    h0_re = h0_re.astype(F32)
    h0_im = h0_im.astype(F32)
    bu_re = bu_re.at[:, 0].add(ar * h0_re - ai * h0_im)
    bu_im = bu_im.at[:, 0].add(ar * h0_im + ai * h0_re)
    a_re = jnp.broadcast_to(ar, bu_re.shape)
    a_im = jnp.broadcast_to(ai, bu_im.shape)

    def combine(e1, e2):
        a1r, a1i, b1r, b1i = e1
        a2r, a2i, b2r, b2i = e2
        return (a2r * a1r - a2i * a1i,
                a2r * a1i + a2i * a1r,
                a2r * b1r - a2i * b1i + b2r,
                a2r * b1i + a2i * b1r + b2i)

    _, _, h_re, h_im = lax.associative_scan(combine, (a_re, a_im, bu_re, bu_im), axis=1)
    y = (jnp.einsum('btgp,ghp->btgh', h_re, c_re.astype(F32))
         - jnp.einsum('btgp,ghp->btgh', h_im, c_im.astype(F32))
         + d_skip.astype(F32).reshape(SSM_GROUPS, SSM_GROUP) * ug)
    return y.reshape(n, t, SSM_WIDTH), h_re[:, -1], h_im[:, -1]


def _short_conv(b_gate, c_gate, h, conv_state, conv_w, conv_b):
    v = c_gate * h
    t = v.shape[1]
    vp = jnp.concatenate([conv_state.astype(v.dtype), v], axis=1)
    y = conv_b
    for tap in range(CONV_K):
        y = y + conv_w[tap] * vp[:, tap:tap + t]
    return b_gate * y, vp[:, vp.shape[1] - (CONV_K - 1):]


def _ssm_conv_layer(x, h0_re, h0_im, conv_state, mem_k, mem_v, w_in, w_out, lam_re, lam_im,
                    log_dt, b_re, b_im, c_re, c_im, d_skip, glu_w, glu_b, conv_w, conv_b, ln_g, ln_b):
    u_a, g_a, b_g, c_g, h_b, g_b, q_m, g_m = _split(x @ w_in, EVEN_SPLITS)
    y_a, h_re, h_im = _s5_scan(u_a, h0_re, h0_im, lam_re, lam_im, log_dt, b_re, b_im, c_re, c_im, d_skip)
    z = jax.nn.gelu(y_a.astype(x.dtype))
    y_a = z * jax.nn.sigmoid(z @ glu_w + glu_b)
    y_b, conv_new = _short_conv(b_g, c_g, h_b, conv_state, conv_w, conv_b)
    y_m = _mem_attend(q_m, mem_k, mem_v)
    mix = jnp.concatenate([y_a * jax.nn.silu(g_a), y_b * jax.nn.silu(g_b), y_m * jax.nn.silu(g_m)], axis=-1)
    x_out = _layer_norm(DN_ALPHA * x + mix @ w_out, ln_g, ln_b)
    return x_out, h_re, h_im, conv_new


def _fox_prompt_attend(q, k, v, lf):
    n, t = q.shape[:2]
    c = jnp.cumsum(lf, axis=1).transpose(0, 2, 1)
    k_pos = jnp.arange(t)
    scale = FOX_HEAD_DIM ** -0.5

    def block(i):
        s0 = i * Q_BLOCK
        qb = lax.dynamic_slice_in_dim(q, s0, Q_BLOCK, axis=1)
        cq = lax.dynamic_slice_in_dim(c, s0, Q_BLOCK, axis=2)
        logits = (jnp.einsum('bqhd,bkhd->bhqk', qb, k).astype(F32) * scale
                  + cq[..., :, None] - c[..., None, :])
        q_pos = s0 + jnp.arange(Q_BLOCK)
        logits = jnp.where(k_pos[None, :] <= q_pos[:, None], logits, -jnp.inf)
        p = jax.nn.softmax(logits, axis=-1).astype(v.dtype)
        return jnp.einsum('bhqk,bkhd->bqhd', p, v)

    out = lax.map(block, jnp.arange(t // Q_BLOCK))
    return out.transpose(1, 0, 2, 3, 4).reshape(n, t, FOX_WIDTH)


def _fox_sample_attend(q, k_new, v_new, lf_new, k_past, v_past, lf_past):
    n, t = q.shape[:2]
    p_len = k_past.shape[1]
    scale = FOX_HEAD_DIM ** -0.5
    lf_all = jnp.concatenate([lf_past.astype(F32), lf_new], axis=1)
    c = jnp.cumsum(lf_all, axis=1).transpose(0, 2, 1)
    bias = c[:, :, p_len:, None] - c[:, :, None, :]
    scores = jnp.concatenate([jnp.einsum('bqhd,bkhd->bhqk', q, k_past),
                              jnp.einsum('bqhd,bkhd->bhqk', q, k_new)], axis=-1)
    logits = scores.astype(F32) * scale + bias
    key_pos = jnp.arange(p_len + t)
    q_pos = p_len + jnp.arange(t)
    logits = jnp.where(key_pos[None, :] <= q_pos[:, None], logits, -jnp.inf)
    p = jax.nn.softmax(logits, axis=-1)
    out = (jnp.einsum('bhqk,bkhd->bqhd', p[..., :p_len].astype(v_past.dtype), v_past)
           + jnp.einsum('bhqk,bkhd->bqhd', p[..., p_len:].astype(v_new.dtype), v_new))
    return out.reshape(n, t, FOX_WIDTH)


def _fox_layer(x, mem_k, mem_v, w_in, b_f, w_out, ln_g, ln_b, past):
    n, t, _ = x.shape
    q, k, v, g, f, q_m, g_m = _split(x @ w_in, ODD_SPLITS)
    q = q.reshape(n, t, FOX_HEADS, FOX_HEAD_DIM)
    k = k.reshape(n, t, FOX_HEADS, FOX_HEAD_DIM)
    v = v.reshape(n, t, FOX_HEADS, FOX_HEAD_DIM)
    lf = jax.nn.log_sigmoid(f.astype(F32) + b_f.astype(F32))
    if past is None:
        y = _fox_prompt_attend(q, k, v, lf)
    else:
        y = _fox_sample_attend(q, k, v, lf, *past)
    y_m = _mem_attend(q_m, mem_k, mem_v)
    mix = jnp.concatenate([y * jax.nn.silu(g), y_m * jax.nn.silu(g_m)], axis=-1)
    x_out = _layer_norm(DN_ALPHA * x + mix @ w_out, ln_g, ln_b)
    return x_out, k, v, lf


def setup_inputs(seed: int = 0) -> dict:
    key = jax.random.key(seed)
    ks = iter(jax.random.split(key, 40))
    nrm = lambda shape: jax.random.normal(next(ks), shape, F32)
    n_pages = PAST_LEN // PAGE_SIZE
    n_used = DEC_BATCH * n_pages
    n_pool = n_used + max(1, n_used // 4)
    page_table = jax.random.permutation(next(ks), n_pool)[:n_used].reshape(DEC_BATCH, n_pages).astype(jnp.int32)
    lam_im = jnp.pi * jnp.arange(SSM_STATE, dtype=F32)
    return {
        "x_prompt": nrm((BATCH, SEQ, D_MODEL)),
        "x_sample": nrm((DEC_BATCH, DEC_SEQ, D_MODEL)),
        "cache_fox_k": nrm((N_FOX_LAYERS, n_pool, PAGE_SIZE, FOX_HEADS, FOX_HEAD_DIM)),
        "cache_fox_v": nrm((N_FOX_LAYERS, n_pool, PAGE_SIZE, FOX_HEADS, FOX_HEAD_DIM)),
        "cache_fox_lf": jax.nn.log_sigmoid(FGATE_BIAS_MEAN + nrm((N_FOX_LAYERS, n_pool, PAGE_SIZE, FOX_HEADS))),
        "cache_mem_k": nrm((DEPTH, DEC_BATCH, MEM_LEN, MEM_HEADS, MEM_HEAD_DIM)),
        "cache_mem_v": nrm((DEPTH, DEC_BATCH, MEM_LEN, MEM_HEADS, MEM_HEAD_DIM)),
        "state_ssm_re": 0.1 * nrm((N_SSM_LAYERS, DEC_BATCH, SSM_GROUPS, SSM_STATE)),
        "state_ssm_im": 0.1 * nrm((N_SSM_LAYERS, DEC_BATCH, SSM_GROUPS, SSM_STATE)),
        "state_conv": nrm((N_SSM_LAYERS, DEC_BATCH, CONV_K - 1, CONV_CH)),
        "page_table": page_table,
        "mem_prompt": nrm((BATCH, MEM_LEN, D_MODEL)),
        "w_in_ssm": nrm((N_SSM_LAYERS, D_MODEL, EVEN_COLS)) * D_MODEL ** -0.5,
        "w_out_ssm": nrm((N_SSM_LAYERS, EVEN_MIX, D_MODEL)) * (EVEN_MIX ** -0.5 * DN_BETA),
        "ssm_lam_re": -0.5 + 0.01 * nrm((N_SSM_LAYERS, SSM_GROUPS, SSM_STATE)),
        "ssm_lam_im": lam_im + 0.01 * nrm((N_SSM_LAYERS, SSM_GROUPS, SSM_STATE)),
        "ssm_log_dt": jax.random.uniform(next(ks), (N_SSM_LAYERS, SSM_GROUPS), F32, math.log(DT_MIN), math.log(DT_MAX)),
        "ssm_b_re": nrm((N_SSM_LAYERS, SSM_GROUPS, SSM_STATE, SSM_GROUP)) * (2 * SSM_GROUP) ** -0.5,
        "ssm_b_im": nrm((N_SSM_LAYERS, SSM_GROUPS, SSM_STATE, SSM_GROUP)) * (2 * SSM_GROUP) ** -0.5,
        "ssm_c_re": nrm((N_SSM_LAYERS, SSM_GROUPS, SSM_GROUP, SSM_STATE)) * SSM_STATE ** -0.5,
        "ssm_c_im": nrm((N_SSM_LAYERS, SSM_GROUPS, SSM_GROUP, SSM_STATE)) * SSM_STATE ** -0.5,
        "ssm_d": nrm((N_SSM_LAYERS, SSM_WIDTH)),
        "glu_w": nrm((N_SSM_LAYERS, SSM_WIDTH, SSM_WIDTH)) * SSM_WIDTH ** -0.5,
        "glu_b": 0.01 * nrm((N_SSM_LAYERS, SSM_WIDTH)),
        "conv_w": nrm((N_SSM_LAYERS, CONV_K, CONV_CH)) * CONV_K ** -0.5,
        "conv_b": 0.01 * nrm((N_SSM_LAYERS, CONV_CH)),
        "w_in_fox": nrm((N_FOX_LAYERS, D_MODEL, ODD_COLS)) * D_MODEL ** -0.5,
        "b_fgate": FGATE_BIAS_MEAN + 0.5 * nrm((N_FOX_LAYERS, FOX_HEADS)),
        "w_out_fox": nrm((N_FOX_LAYERS, ODD_MIX, D_MODEL)) * (ODD_MIX ** -0.5 * DN_BETA),
        "w_mem_k": nrm((DEPTH, D_MODEL, MEM_WIDTH)) * D_MODEL ** -0.5,
        "w_mem_v": nrm((DEPTH, D_MODEL, MEM_WIDTH)) * D_MODEL ** -0.5,
        "ln_g": 1.0 + 0.01 * nrm((DEPTH, D_MODEL)),
        "ln_b": 0.01 * nrm((DEPTH, D_MODEL)),
    }


def reference(x_prompt, x_sample, cache_fox_k, cache_fox_v, cache_fox_lf, cache_mem_k, cache_mem_v,
              state_ssm_re, state_ssm_im, state_conv, page_table, mem_prompt,
              w_in_ssm, w_out_ssm, ssm_lam_re, ssm_lam_im, ssm_log_dt, ssm_b_re, ssm_b_im,
              ssm_c_re, ssm_c_im, ssm_d, glu_w, glu_b, conv_w, conv_b,
              w_in_fox, b_fgate, w_out_fox, w_mem_k, w_mem_v, ln_g, ln_b):
    xp, xs = x_prompt, x_sample
    n_p, n_s = xp.shape[0], xs.shape[0]
    fk_p, fv_p, flf_p, fk_s, fv_s, flf_s = [], [], [], [], [], []
    sr_p, si_p, cv_p, sr_s, si_s, cv_s = [], [], [], [], [], []
    mk_list, mv_list = [], []
    for layer in range(DEPTH):
        i = layer // 2
        mk_p = (mem_prompt @ w_mem_k[layer]).reshape(n_p, MEM_LEN, MEM_HEADS, MEM_HEAD_DIM)
        mv_p = (mem_prompt @ w_mem_v[layer]).reshape(n_p, MEM_LEN, MEM_HEADS, MEM_HEAD_DIM)
        mk_list.append(mk_p)
        mv_list.append(mv_p)
        if layer % 2 == 0:
            params = (w_in_ssm[i], w_out_ssm[i], ssm_lam_re[i], ssm_lam_im[i], ssm_log_dt[i],
                      ssm_b_re[i], ssm_b_im[i], ssm_c_re[i], ssm_c_im[i], ssm_d[i],
                      glu_w[i], glu_b[i], conv_w[i], conv_b[i], ln_g[layer], ln_b[layer])
            zero_state = jnp.zeros((n_p, SSM_GROUPS, SSM_STATE), F32)
            zero_conv = jnp.zeros((n_p, CONV_K - 1, CONV_CH), xp.dtype)
            xp, hr, hi, cv = _ssm_conv_layer(xp, zero_state, zero_state, zero_conv, mk_p, mv_p, *params)
            sr_p.append(hr)
            si_p.append(hi)
            cv_p.append(cv)
            xs, hr, hi, cv = _ssm_conv_layer(xs, state_ssm_re[i], state_ssm_im[i], state_conv[i],
                                             cache_mem_k[layer], cache_mem_v[layer], *params)
            sr_s.append(hr)
            si_s.append(hi)
            cv_s.append(cv)
        else:
            params = (w_in_fox[i], b_fgate[i], w_out_fox[i], ln_g[layer], ln_b[layer])
            xp, k, v, lf = _fox_layer(xp, mk_p, mv_p, *params, past=None)
            fk_p.append(k)
            fv_p.append(v)
            flf_p.append(lf)
            past = (cache_fox_k[i, page_table].reshape(n_s, -1, FOX_HEADS, FOX_HEAD_DIM),
                    cache_fox_v[i, page_table].reshape(n_s, -1, FOX_HEADS, FOX_HEAD_DIM),
                    cache_fox_lf[i, page_table].reshape(n_s, -1, FOX_HEADS))
            xs, k, v, lf = _fox_layer(xs, cache_mem_k[layer], cache_mem_v[layer], *params, past=past)
            fk_s.append(k)
            fv_s.append(v)
            flf_s.append(lf)
    return (xp, xs,
            jnp.stack(fk_p), jnp.stack(fv_p), jnp.stack(flf_p),
            jnp.stack(fk_s), jnp.stack(fv_s), jnp.stack(flf_s),
            jnp.stack(sr_p), jnp.stack(si_p), jnp.stack(cv_p),
            jnp.stack(sr_s), jnp.stack(si_s), jnp.stack(cv_s),
            jnp.stack(mk_list), jnp.stack(mv_list))
```

```python
from functools import partial

import jax
import jax.numpy as jnp
from jax import lax
from jax.experimental import pallas as pl
from jax.experimental.pallas import tpu as pltpu

F32 = jnp.float32
BF16 = jnp.bfloat16

LN_EPS = 1e-5
LANES = 128
SUBLANES = 8
VMEM_LIMIT_BYTES = 56 * 1024 * 1024
NT_DIMS = (((1,), (1,)), ((), ()))


def _params(*sem):
    return pltpu.CompilerParams(dimension_semantics=sem, vmem_limit_bytes=VMEM_LIMIT_BYTES)


def _silu(g):
    return g * jax.nn.sigmoid(g)


def _mm_body(*refs, n_pairs, alpha, has_res):
    out_ref = refs[-1]
    acc = None
    for p in range(n_pairs):
        x = refs[2 * p][...].astype(BF16)
        w = refs[2 * p + 1][...].astype(BF16)
        d = jnp.dot(x, w, preferred_element_type=F32)
        acc = d if acc is None else acc + d
    if has_res:
        acc = alpha * refs[2 * n_pairs][...] + acc
    out_ref[...] = acc.astype(out_ref.dtype)


def _matmul(pairs, n_out, *, tm, tn, res=None, alpha=1.0, out_dtype=F32, name="mm"):
    m = pairs[0][0].shape[0]
    tm, tn = min(tm, m), min(tn, n_out)
    assert m % tm == 0 and n_out % tn == 0
    in_specs, args = [], []
    for (x, xkb, w, layer, wkb, col0, kdim) in pairs:
        assert col0 % tn == 0
        in_specs.append(pl.BlockSpec((tm, kdim), lambda i, j, xkb=xkb: (i, xkb)))
        in_specs.append(pl.BlockSpec((None, kdim, tn),
                                     lambda i, j, layer=layer, wkb=wkb, cb=col0 // tn: (layer, wkb, cb + j)))
        args += [x, w]
    if res is not None:
        in_specs.append(pl.BlockSpec((tm, tn), lambda i, j: (i, j)))
        args.append(res)
    return pl.pallas_call(
        partial(_mm_body, n_pairs=len(pairs), alpha=alpha, has_res=res is not None),
        grid=(m // tm, n_out // tn),
        in_specs=in_specs,
        out_specs=pl.BlockSpec((tm, tn), lambda i, j: (i, j)),
        out_shape=jax.ShapeDtypeStruct((m, n_out), out_dtype),
        compiler_params=_params("parallel", "arbitrary"),
        name=name,
    )(*args)


def _memproj_body(x_ref, w_ref, o_ref):
    o_ref[...] = jnp.dot(x_ref[...].astype(BF16), w_ref[...].astype(BF16), preferred_element_type=F32)


def _memproj(x, w):
    nl, d, n = w.shape
    r = x.shape[0]
    return pl.pallas_call(
        _memproj_body,
        grid=(nl,),
        in_specs=[pl.BlockSpec((r, d), lambda l: (0, 0)), pl.BlockSpec((None, d, n), lambda l: (l, 0, 0))],
        out_specs=pl.BlockSpec((None, r, n), lambda l: (l, 0, 0)),
        out_shape=jax.ShapeDtypeStruct((nl, r, n), F32),
        compiler_params=_params("arbitrary"),
        name="memproj",
    )(x, w)


def _ln_body(y_ref, g_ref, b_ref, xf_ref, xb_ref):
    y = y_ref[...]
    mu = jnp.mean(y, axis=-1, keepdims=True)
    d = y - mu
    var = jnp.mean(d * d, axis=-1, keepdims=True)
    out = d * lax.rsqrt(var + LN_EPS) * g_ref[...] + b_ref[...]
    xf_ref[...] = out
    xb_ref[...] = out.astype(BF16)


def _layer_norm(y, g, b, layer, *, tm):
    m, d = y.shape
    tm = min(tm, m)
    g3 = g.reshape(g.shape[0], 1, d)
    b3 = b.reshape(b.shape[0], 1, d)
    vec = pl.BlockSpec((None, 1, d), lambda i: (layer, 0, 0))
    row = pl.BlockSpec((tm, d), lambda i: (i, 0))
    return pl.pallas_call(
        _ln_body,
        grid=(m // tm,),
        in_specs=[row, vec, vec],
        out_specs=[row, row],
        out_shape=[jax.ShapeDtypeStruct((m, d), F32), jax.ShapeDtypeStruct((m, d), BF16)],
        compiler_params=_params("parallel"),
        name="layernorm",
    )(y, g3, b3)


def _s5_discretise(lam_re, lam_im, log_dt, b_re, b_im, c_re, c_im, d_skip):
    g, p, h = b_re.shape
    gpc = LANES // h
    ncc = g // gpc
    lam_re, lam_im = lam_re.astype(F32), lam_im.astype(F32)
    dt = jnp.exp(log_dt.astype(F32))[:, None]
    mag = jnp.exp(lam_re * dt)
    ar = mag * jnp.cos(lam_im * dt)
    ai = mag * jnp.sin(lam_im * dt)
    nr = ar - 1.0
    den = jnp.square(lam_re) + jnp.square(lam_im)
    cr = ((nr * lam_re + ai * lam_im) / den)[..., None]
    ci = ((ai * lam_re - nr * lam_im) / den)[..., None]
    b_re, b_im = b_re.astype(F32), b_im.astype(F32)
    bb_re = cr * b_re - ci * b_im
    bb_im = cr * b_im + ci * b_re
    eye = jnp.eye(gpc, dtype=F32)

    def pack_in(bb):
        return jnp.einsum('cgph,gk->cghkp', bb.reshape(ncc, gpc, p, h), eye).reshape(ncc, gpc * h, gpc * p)

    def pack_out(c):
        return jnp.einsum('cghp,gk->cgpkh', c.astype(F32).reshape(ncc, gpc, h, p), eye).reshape(ncc, gpc * p, gpc * h)

    return dict(a_re=ar.reshape(ncc, 1, gpc * p), a_im=ai.reshape(ncc, 1, gpc * p),
                wb_re=pack_in(bb_re), wb_im=pack_in(bb_im), wc_re=pack_out(c_re), wc_im=pack_out(c_im),
                d=d_skip.astype(F32).reshape(ncc, 1, gpc * h))


def _s5_body(u_ref, h0r_ref, h0i_ref, ar_ref, ai_ref, wbr_ref, wbi_ref, wcr_ref, wci_ref, d_ref,
             y_ref, hr_out, hi_out, hre, him, *, steps, chain):
    u = u_ref[...]
    ub = u.astype(BF16)
    hre[...] = jnp.dot(ub, wbr_ref[...].astype(BF16), preferred_element_type=F32)
    him[...] = jnp.dot(ub, wbi_ref[...].astype(BF16), preferred_element_type=F32)
    ns = hre.shape[1]
    ar = jnp.broadcast_to(ar_ref[...], (SUBLANES, ns))
    ai = jnp.broadcast_to(ai_ref[...], (SUBLANES, ns))

    def rows_of(t):
        return pl.ds(pl.multiple_of(t * SUBLANES, SUBLANES), SUBLANES)

    def scan_step(t, c):
        hr, hi, pr, pi = c
        rows = rows_of(t)
        nr = ar * hr - ai * hi + hre[rows, :]
        ni = ar * hi + ai * hr + him[rows, :]
        hre[rows, :] = nr
        him[rows, :] = ni
        return nr, ni, ar * pr - ai * pi, ar * pi + ai * pr

    one = jnp.ones((SUBLANES, ns), F32)
    fr, fi, alr, ali = lax.fori_loop(0, steps, scan_step, (h0r_ref[...], h0i_ref[...], one, jnp.zeros_like(one)))

    if chain:
        sub = lax.broadcasted_iota(jnp.int32, (SUBLANES, ns), 0)
        sr, si = fr, fi
        for c in range(1, SUBLANES):
            pr, pi = pltpu.roll(sr, 1, 0), pltpu.roll(si, 1, 0)
            sr = jnp.where(sub == c, alr * pr - ali * pi + fr, sr)
            si = jnp.where(sub == c, alr * pi + ali * pr + fi, si)
        cin_r = jnp.where(sub == 0, 0.0, pltpu.roll(sr, 1, 0))
        cin_i = jnp.where(sub == 0, 0.0, pltpu.roll(si, 1, 0))

        def fix_step(t, c):
            pr, pi = c
            rows = rows_of(t)
            hre[rows, :] = hre[rows, :] + (pr * cin_r - pi * cin_i)
            him[rows, :] = him[rows, :] + (pr * cin_i + pi * cin_r)
            return ar * pr - ai * pi, ar * pi + ai * pr

        lax.fori_loop(0, steps, fix_step, (ar, ai))
        fr, fi = sr, si

    hr_out[...] = fr
    hi_out[...] = fi
    y = (jnp.dot(hre[...].astype(BF16), wcr_ref[...].astype(BF16), preferred_element_type=F32)
         - jnp.dot(him[...].astype(BF16), wci_ref[...].astype(BF16), preferred_element_type=F32)
         + d_ref[...] * u)
    y_ref[...] = y


def _s5(u, h0_re, h0_im, prm, *, chain):
    nb, rows, w = u.shape
    steps = rows // SUBLANES
    ncc, cw, ns = prm["wb_re"].shape
    assert w == ncc * cw and cw == LANES
    seq = pl.BlockSpec((None, rows, cw), lambda b, c: (b, 0, c))
    st = pl.BlockSpec((None, SUBLANES, ns), lambda b, c: (b, 0, c))

    def par(shape):
        return pl.BlockSpec((None,) + shape, lambda b, c: (c, 0, 0))

    return pl.pallas_call(
        partial(_s5_body, steps=steps, chain=chain),
        grid=(nb, ncc),
        in_specs=[seq, st, st, par((1, ns)), par((1, ns)), par((cw, ns)), par((cw, ns)),
                  par((ns, cw)), par((ns, cw)), par((1, cw))],
        out_specs=[seq, st, st],
        out_shape=[jax.ShapeDtypeStruct((nb, rows, w), F32),
                   jax.ShapeDtypeStruct((nb, SUBLANES, ncc * ns), F32),
                   jax.ShapeDtypeStruct((nb, SUBLANES, ncc * ns), F32)],
        scratch_shapes=[pltpu.VMEM((rows, ns), F32), pltpu.VMEM((rows, ns), F32)],
        compiler_params=_params("parallel", "arbitrary"),
        name="s5_scan",
    )(u, h0_re, h0_im, prm["a_re"], prm["a_im"], prm["wb_re"], prm["wb_im"], prm["wc_re"], prm["wc_im"], prm["d"])


def _glu_body(yfull_ref, yblk_ref, ga_ref, w_ref, b_ref, o_ref, zb):
    @pl.when(pl.program_id(1) == 0)
    def _():
        zb[...] = jax.nn.gelu(yfull_ref[...]).astype(BF16)

    glu = jnp.dot(zb[...], w_ref[...].astype(BF16), preferred_element_type=F32) + b_ref[...]
    z = jax.nn.gelu(yblk_ref[...])
    o_ref[...] = (z * jax.nn.sigmoid(glu) * _silu(ga_ref[...])).astype(o_ref.dtype)


def _glu_gate(y, z_all, ga_col0, glu_w, glu_b, layer, *, tm, tn, out_dtype):
    m, w = y.shape
    tm, tn = min(tm, m), min(tn, w)
    assert ga_col0 % tn == 0
    gb3 = glu_b.reshape(glu_b.shape[0], 1, w)
    return pl.pallas_call(
        _glu_body,
        grid=(m // tm, w // tn),
        in_specs=[pl.BlockSpec((tm, w), lambda i, j: (i, 0)),
                  pl.BlockSpec((tm, tn), lambda i, j: (i, j)),
                  pl.BlockSpec((tm, tn), lambda i, j: (i, ga_col0 // tn + j)),
                  pl.BlockSpec((None, w, tn), lambda i, j: (layer, 0, j)),
                  pl.BlockSpec((None, 1, tn), lambda i, j: (layer, 0, j))],
        out_specs=pl.BlockSpec((tm, tn), lambda i, j: (i, j)),
        out_shape=jax.ShapeDtypeStruct((m, w), out_dtype),
        scratch_shapes=[pltpu.VMEM((tm, w), BF16)],
        compiler_params=_params("parallel", "arbitrary"),
        name="glu_gate",
    )(y, y, z_all, glu_w, gb3)


def _conv_body(bg_ref, cg_ref, hb_ref, gb_ref, cs_ref, w_ref, b_ref, o_ref, cn_ref, prev, *, tiles_per_seq):
    @pl.when(pl.program_id(1) % tiles_per_seq == 0)
    def _():
        prev[...] = cs_ref[...]

    v = cg_ref[...] * hb_ref[...]
    tm = v.shape[0]
    p2, p1 = prev[0:1, :], prev[1:2, :]
    row = lax.broadcasted_iota(jnp.int32, v.shape, 0)
    v1 = jnp.where(row == 0, p1, pltpu.roll(v, 1, 0))
    v2 = jnp.where(row == 0, p2, jnp.where(row == 1, p1, pltpu.roll(v, 2, 0)))
    y = b_ref[...] + w_ref[0:1, :] * v2 + w_ref[1:2, :] * v1 + w_ref[2:3, :] * v
    o_ref[...] = (bg_ref[...] * y * _silu(gb_ref[...])).astype(o_ref.dtype)
    last = v[tm - 2:tm, :]
    prev[...] = last
    cn_ref[...] = last


def _conv_gate(z_all, col0, c, conv_state, conv_w, conv_b, layer, seq_len, *, tm, ct):
    m = z_all.shape[0]
    tm, ct = min(tm, seq_len), min(ct, c)
    assert seq_len % tm == 0 and all(o % ct == 0 for o in col0) and conv_w.shape[1] == 3
    tps = seq_len // tm
    n = m // seq_len
    cb3 = conv_b.reshape(conv_b.shape[0], 1, c)

    def zcol(o):
        return pl.BlockSpec((tm, ct), lambda j, i, o=o: (i, o // ct + j))

    st = pl.BlockSpec((None, 2, ct), lambda j, i: (i // tps, 0, j))
    return pl.pallas_call(
        partial(_conv_body, tiles_per_seq=tps),
        grid=(c // ct, m // tm),
        in_specs=[zcol(col0[0]), zcol(col0[1]), zcol(col0[2]), zcol(col0[3]), st,
                  pl.BlockSpec((None, 3, ct), lambda j, i: (layer, 0, j)),
                  pl.BlockSpec((None, 1, ct), lambda j, i: (layer, 0, j))],
        out_specs=[pl.BlockSpec((tm, ct), lambda j, i: (i, j)), st],
        out_shape=[jax.ShapeDtypeStruct((m, c), BF16), jax.ShapeDtypeStruct((n, 2, c), F32)],
        scratch_shapes=[pltpu.VMEM((2, ct), F32)],
        compiler_params=_params("parallel", "arbitrary"),
        name="conv_gate",
    )(z_all, z_all, z_all, z_all, conv_state, conv_w, cb3)


def _conv_step_body(bg_ref, cg_ref, hb_ref, gb_ref, cs_ref, w_ref, b_ref, o_ref, cn_ref, *, steps):
    vp = [cs_ref[0], cs_ref[1]] + [cg_ref[t] * hb_ref[t] for t in range(steps)]
    for t in range(steps):
        y = b_ref[...] + w_ref[0:1, :] * vp[t] + w_ref[1:2, :] * vp[t + 1] + w_ref[2:3, :] * vp[t + 2]
        o_ref[t] = bg_ref[t] * y * _silu(gb_ref[t])
    cn_ref[0] = vp[steps]
    cn_ref[1] = vp[steps + 1]


def _conv_gate_step(bg, cg, hb, gb, conv_state_t, conv_w, conv_b, layer, *, ct):
    steps, n, c = bg.shape
    ct = min(ct, c)
    assert conv_w.shape[1] == 3
    cb3 = conv_b.reshape(conv_b.shape[0], 1, c)
    blk = pl.BlockSpec((steps, n, ct), lambda j: (0, 0, j))
    st = pl.BlockSpec((2, n, ct), lambda j: (0, 0, j))
    return pl.pallas_call(
        partial(_conv_step_body, steps=steps),
        grid=(c // ct,),
        in_specs=[blk, blk, blk, blk, st,
                  pl.BlockSpec((None, 3, ct), lambda j: (layer, 0, j)),
                  pl.BlockSpec((None, 1, ct), lambda j: (layer, 0, j))],
        out_specs=[blk, st],
        out_shape=[jax.ShapeDtypeStruct((steps, n, c), F32), jax.ShapeDtypeStruct((2, n, c), F32)],
        compiler_params=_params("parallel"),
        name="conv_gate_step",
    )(bg, cg, hb, gb, conv_state_t, conv_w, cb3)


def _mem_body(q_ref, g_ref, k_ref, v_ref, o_ref, *, nh, hd, rows_per_seq, keys_per_seq):
    scale = hd ** -0.5
    tq, nk = q_ref.shape[0], k_ref.shape[0]
    if rows_per_seq is not None:
        rseq = lax.broadcasted_iota(jnp.int32, (tq, nk), 0) // rows_per_seq
        kseq = lax.broadcasted_iota(jnp.int32, (tq, nk), 1) // keys_per_seq
        own = rseq == kseq
    for h in range(nh):
        sl = slice(h * hd, (h + 1) * hd)
        q = q_ref[:, sl].astype(BF16)
        k = k_ref[:, sl].astype(BF16)
        v = v_ref[:, sl].astype(BF16)
        s = lax.dot_general(q, k, NT_DIMS, preferred_element_type=F32) * scale
        if rows_per_seq is not None:
            s = jnp.where(own, s, -jnp.inf)
        e = jnp.exp(s - jnp.max(s, axis=-1, keepdims=True))
        p = e / jnp.sum(e, axis=-1, keepdims=True)
        o = jnp.dot(p.astype(BF16), v, preferred_element_type=F32)
        o_ref[:, sl] = (o * _silu(g_ref[:, sl])).astype(o_ref.dtype)


def _mem_attend(z_all, q_col0, g_col0, mem_k, mem_v, layer, seq_len, nh, *, tq, joint, out_dtype):
    m = z_all.shape[0]
    mw = mem_k.shape[2]
    n = m // seq_len
    mem_len = mem_k.shape[1] // n
    assert q_col0 % mw == 0 and g_col0 % mw == 0
    if joint:
        tq, kv_rows, rps = m, n * mem_len, seq_len
        kv_idx = lambda i: (layer, 0, 0)
    else:
        tq = min(tq, seq_len)
        assert seq_len % tq == 0
        tps = seq_len // tq
        kv_rows, rps = mem_len, None
        kv_idx = lambda i: (layer, i // tps, 0)
    kv = pl.BlockSpec((None, kv_rows, mw), kv_idx)
    return pl.pallas_call(
        partial(_mem_body, nh=nh, hd=mw // nh, rows_per_seq=rps, keys_per_seq=mem_len),
        grid=(m // tq,),
        in_specs=[pl.BlockSpec((tq, mw), lambda i: (i, q_col0 // mw)),
                  pl.BlockSpec((tq, mw), lambda i: (i, g_col0 // mw)), kv, kv],
        out_specs=pl.BlockSpec((tq, mw), lambda i: (i, 0)),
        out_shape=jax.ShapeDtypeStruct((m, mw), out_dtype),
        compiler_params=_params("parallel"),
        name="mem_attend",
    )(z_all, z_all, mem_k, mem_v)


def _log_sigmoid(x):
    return jnp.minimum(x, 0.0) - jnp.log1p(jnp.exp(-jnp.abs(x)))


def _lf_body(f_ref, bf_ref, lf_ref, c_ref, ct_ref, *, tk):
    lf = _log_sigmoid(f_ref[...] + bf_ref[...])
    lf_ref[...] = lf
    t = lf.shape[0]
    row = lax.broadcasted_iota(jnp.int32, lf.shape, 0)
    c, s = lf, 1
    while s < t:
        c = c + jnp.where(row >= s, pltpu.roll(c, s, 0), 0.0)
        s *= 2
    c_ref[...] = c
    for j in range(t // tk):
        ct_ref[j] = c[j * tk:(j + 1) * tk, :].T


def _forget_gates(z_tail, f_col0, b_f128, layer, n, seq_len, *, tk):
    assert f_col0 % LANES == 0
    tk = min(tk, seq_len)
    z3 = z_tail.reshape(n, seq_len, z_tail.shape[1])
    blk = pl.BlockSpec((None, seq_len, LANES), lambda b: (b, 0, 0))
    return pl.pallas_call(
        partial(_lf_body, tk=tk),
        grid=(n,),
        in_specs=[pl.BlockSpec((None, seq_len, LANES), lambda b: (b, 0, f_col0 // LANES)),
                  pl.BlockSpec((None, 1, LANES), lambda b: (layer, 0, 0))],
        out_specs=[blk, blk, pl.BlockSpec((None, seq_len // tk, LANES, tk), lambda b: (b, 0, 0, 0))],
        out_shape=[jax.ShapeDtypeStruct((n, seq_len, LANES), F32), jax.ShapeDtypeStruct((n, seq_len, LANES), F32),
                   jax.ShapeDtypeStruct((n, seq_len // tk, LANES, tk), F32)],
        compiler_params=_params("parallel"),
        name="forget_gates",
    )(z3, b_f128)


def _lf_step_body(f_ref, bf_ref, lf_ref):
    lf_ref[...] = _log_sigmoid(f_ref[...] + bf_ref[...])


def _forget_gates_step(z_tail, f_col0, b_f128, layer):
    m = z_tail.shape[0]
    return pl.pallas_call(
        _lf_step_body,
        grid=(1,),
        in_specs=[pl.BlockSpec((m, LANES), lambda i: (0, f_col0 // LANES)),
                  pl.BlockSpec((None, 1, LANES), lambda i: (layer, 0, 0))],
        out_specs=pl.BlockSpec((m, LANES), lambda i: (0, 0)),
        out_shape=jax.ShapeDtypeStruct((m, LANES), F32),
        compiler_params=_params("arbitrary"),
        name="forget_gates_step",
    )(z_tail, b_f128)


def _fox_body(q_ref, k_ref, v_ref, g_ref, c_ref, ck_ref, o_ref, *, tq, tk, hd):
    h = pl.program_id(1)
    qi = pl.program_id(2)
    scale = hd ** -0.5
    q = q_ref[...].astype(BF16)
    lane = lax.broadcasted_iota(jnp.int32, c_ref.shape, 1)
    cq = jnp.sum(jnp.where(lane == h, c_ref[...], 0.0), axis=1, keepdims=True)
    qpos = qi * tq + lax.broadcasted_iota(jnp.int32, (tq, tk), 0)
    koff = lax.broadcasted_iota(jnp.int32, (tq, tk), 1)

    def kv_step(j, carry):
        m, l, acc = carry
        ks = pl.ds(pl.multiple_of(j * tk, tk), tk)
        k = k_ref[ks, :].astype(BF16)
        v = v_ref[ks, :].astype(BF16)
        ck = ck_ref[j, pl.ds(h, 1), :]
        s = lax.dot_general(q, k, NT_DIMS, preferred_element_type=F32) * scale + (cq - ck)
        s = jnp.where(j * tk + koff <= qpos, s, -jnp.inf)
        m_new = jnp.maximum(m, jnp.max(s, axis=1, keepdims=True))
        a = jnp.exp(m - m_new)
        p = jnp.exp(s - m_new)
        l = a * l + jnp.sum(p, axis=1, keepdims=True)
        acc = a * acc + jnp.dot(p.astype(BF16), v, preferred_element_type=F32)
        return m_new, l, acc

    n_kv = (qi * tq + tq + tk - 1) // tk
    init = (jnp.full((tq, 1), -jnp.inf, F32), jnp.zeros((tq, 1), F32), jnp.zeros((tq, hd), F32))
    _, l, acc = lax.fori_loop(0, n_kv, kv_step, init)
    o_ref[...] = (acc / l * _silu(g_ref[...])).astype(o_ref.dtype)


def _fox_attend(zq, zk, zv, zg, c, ct, n, seq_len, nh, hd, *, tq, tk):
    tq = min(tq, seq_len)
    assert seq_len % tq == 0 and seq_len % tk == 0 and ct.shape[3] == tk
    r3 = lambda a: a.reshape(n, seq_len, nh * hd)
    qblk = pl.BlockSpec((None, tq, hd), lambda b, h, i: (b, i, h))
    kvblk = pl.BlockSpec((None, seq_len, hd), lambda b, h, i: (b, 0, h))
    return pl.pallas_call(
        partial(_fox_body, tq=tq, tk=tk, hd=hd),
        grid=(n, nh, seq_len // tq),
        in_specs=[qblk, kvblk, kvblk, qblk,
                  pl.BlockSpec((None, tq, LANES), lambda b, h, i: (b, i, 0)),
                  pl.BlockSpec((None, seq_len // tk, LANES, tk), lambda b, h, i: (b, 0, 0, 0))],
        out_specs=qblk,
        out_shape=jax.ShapeDtypeStruct((n, seq_len, nh * hd), BF16),
        compiler_params=_params("parallel", "parallel", "arbitrary"),
        name="fox_attend",
    )(r3(zq), r3(zk), r3(zv), r3(zg), c, ct).reshape(n * seq_len, nh * hd)


def _fox_step_body(pt_ref, q_ref, kn_ref, vn_ref, lfn_ref, kc_ref, vc_ref, lfc_ref, o_ref,
                   qbd, m_s, l_s, acc, run, *, steps, nh, hd, n_pages):
    p = pl.program_id(1)
    fw = nh * hd
    ps = kc_ref.shape[0]
    scale = hd ** -0.5
    head_of_col = lax.broadcasted_iota(jnp.int32, (nh, fw), 1) // hd
    own_head = head_of_col == lax.broadcasted_iota(jnp.int32, (nh, fw), 0)

    def attend(kb, vb, bias):
        s = lax.dot_general(qbd[...], kb, NT_DIMS, preferred_element_type=F32) * scale + bias
        m_old = m_s[...]
        m_new = jnp.maximum(m_old, jnp.max(s, axis=1, keepdims=True))
        a = jnp.exp(m_old - m_new)
        e = jnp.exp(s - m_new)
        l_s[...] = a * l_s[...] + jnp.sum(e, axis=1, keepdims=True)
        acc[...] = a * acc[...] + jnp.dot(e.astype(BF16), vb, preferred_element_type=F32)
        m_s[...] = m_new

    @pl.when(p == 0)
    def _():
        q = q_ref[...]
        qbd[...] = jnp.concatenate(
            [jnp.where(own_head, jnp.broadcast_to(q[t:t + 1, :], (nh, fw)), 0.0) for t in range(steps)],
            axis=0).astype(BF16)
        m_s[...] = jnp.full(m_s.shape, -jnp.inf, F32)
        l_s[...] = jnp.zeros(l_s.shape, F32)
        acc[...] = jnp.zeros(acc.shape, F32)
        run[...] = jnp.zeros(run.shape, F32)
        lfn = lfn_ref[...]
        lane = lax.broadcasted_iota(jnp.int32, lfn.shape, 1)
        cn, s = lfn, 1
        while s < steps:
            cn = cn + jnp.where(lane >= s, pltpu.roll(cn, s, 1), 0.0)
            s *= 2
        bias = jnp.concatenate([-cn] * steps, axis=0)
        key = lax.broadcasted_iota(jnp.int32, bias.shape, 1)
        qstep = lax.broadcasted_iota(jnp.int32, bias.shape, 0) // nh
        bias = jnp.where(key <= qstep, bias, -jnp.inf)
        attend(kn_ref[...].astype(BF16), vn_ref[...].astype(BF16), bias)

    lf = lfc_ref[...]
    lane = lax.broadcasted_iota(jnp.int32, lf.shape, 1)
    sfx, s = lf, 1
    while s < ps:
        sfx = sfx + jnp.where(lane + s < ps, pltpu.roll(sfx, ps - s, 1), 0.0)
        s *= 2
    after = sfx - lf + run[...]
    run[...] = run[...] + jnp.sum(lf, axis=1, keepdims=True)
    attend(kc_ref[...].astype(BF16), vc_ref[...].astype(BF16), jnp.concatenate([after] * steps, axis=0))

    @pl.when(p == n_pages - 1)
    def _():
        o = acc[...] / l_s[...]
        o_ref[...] = jnp.concatenate(
            [jnp.sum(jnp.where(own_head, o[t * nh:(t + 1) * nh, :], 0.0), axis=0, keepdims=True)
             for t in range(steps)], axis=0)


def _fox_attend_step(q, k_new_pad, v_new_pad, lf_new_t, kc, vc, lfc_t, page_table, layer, nh, hd):
    n, steps, fw = q.shape
    ps = kc.shape[2]
    n_pages = page_table.shape[1]
    rows = steps * nh
    assert ps % LANES == 0 and steps <= ps
    per_seq = lambda shape: pl.BlockSpec((None,) + shape, lambda b, p, pt: (b, 0, 0))
    page = lambda shape: pl.BlockSpec((None, None) + shape,
                                      lambda b, p, pt: (layer, pt[b * n_pages + n_pages - 1 - p], 0, 0))
    grid_spec = pltpu.PrefetchScalarGridSpec(
        num_scalar_prefetch=1,
        grid=(n, n_pages),
        in_specs=[per_seq((steps, fw)), per_seq((ps, fw)), per_seq((ps, fw)), per_seq((nh, ps)),
                  page((ps, fw)), page((ps, fw)), page((nh, ps))],
        out_specs=per_seq((steps, fw)),
        scratch_shapes=[pltpu.VMEM((rows, fw), BF16), pltpu.VMEM((rows, 1), F32), pltpu.VMEM((rows, 1), F32),
                        pltpu.VMEM((rows, fw), F32), pltpu.VMEM((nh, 1), F32)],
    )
    return pl.pallas_call(
        partial(_fox_step_body, steps=steps, nh=nh, hd=hd, n_pages=n_pages),
        grid_spec=grid_spec,
        out_shape=jax.ShapeDtypeStruct((n, steps, fw), F32),
        compiler_params=_params("arbitrary", "arbitrary"),
        name="fox_attend_step",
    )(page_table.reshape(-1), q, k_new_pad, v_new_pad, lf_new_t, kc, vc, lfc_t)


def _offsets(sizes):
    out, o = [], 0
    for s in sizes:
        out.append(o)
        o += s
    return out, o


def _ssm_conv_layer(xf, xb, seq_len, state_re, state_im, conv_state, mem_k, mem_v, w_in, w_out, s5p, glu_w, glu_b,
                    conv_w, conv_b, ln_g, ln_b, i, layer, alpha, nh_mem, *, decode):
    m, d = xf.shape
    n = m // seq_len
    w, c, mw = glu_w.shape[1], conv_w.shape[2], mem_k.shape[2]
    (o_ua, o_ga, o_bg, o_cg, o_hb, o_gb, o_qm, o_gm), n_cols = _offsets((w, w, c, c, c, c, mw, mw))
    z = _matmul([(xb, 0, w_in, i, 0, 0, d)], n_cols, tm=1024, tn=512, name="in_proj_ssm")
    sps = seq_len if decode else seq_len // SUBLANES
    if decode:
        to_scan = lambda a: a.reshape(n, seq_len, -1).transpose(1, 0, 2)
        u = to_scan(z[:, o_ua:o_ua + w]).reshape(1, m, w)
        h0_re, h0_im = state_re.reshape(1, n, -1), state_im.reshape(1, n, -1)
    else:
        u = z[:, o_ua:o_ua + w].reshape(n, SUBLANES, sps, w).transpose(0, 2, 1, 3).reshape(n, seq_len, w)
        pad = lambda s: jnp.pad(s.reshape(n, 1, -1), ((0, 0), (0, SUBLANES - 1), (0, 0)))
        h0_re, h0_im = pad(state_re), pad(state_im)
    y, h_re, h_im = _s5(u, h0_re, h0_im, s5p, chain=not decode)
    if decode:
        y = y.reshape(seq_len, n, w).transpose(1, 0, 2).reshape(m, w)
        h_re, h_im = h_re[0], h_im[0]
    else:
        y = y.reshape(n, sps, SUBLANES, w).transpose(0, 2, 1, 3).reshape(m, w)
        h_re, h_im = h_re[:, SUBLANES - 1], h_im[:, SUBLANES - 1]
    mix_dtype = F32 if decode else BF16
    mix_a = _glu_gate(y, z, o_ga, glu_w, glu_b, i, tm=512, tn=512, out_dtype=mix_dtype)
    if decode:
        bg, cg, hb, gb = (to_scan(z[:, o:o + c]) for o in (o_bg, o_cg, o_hb, o_gb))
        mix_b, conv_new = _conv_gate_step(bg, cg, hb, gb, conv_state.transpose(1, 0, 2), conv_w, conv_b, i, ct=512)
        mix_b = mix_b.transpose(1, 0, 2).reshape(m, c)
        conv_new = conv_new.transpose(1, 0, 2)
    else:
        mix_b, conv_new = _conv_gate(z, (o_bg, o_cg, o_hb, o_gb), c, conv_state, conv_w, conv_b, i, seq_len,
                                     tm=512, ct=512)
    mix_m = _mem_attend(z, o_qm, o_gm, mem_k, mem_v, layer, seq_len, nh_mem, tq=512, joint=decode,
                        out_dtype=mix_dtype)
    y_out = _matmul([(mix_a, 0, w_out, i, 0, 0, w), (mix_b, 0, w_out, i, w // c, 0, c),
                     (mix_m, 0, w_out, i, (w + c) // mw, 0, mw)],
                    d, tm=1024, tn=256, res=xf, alpha=alpha, name="out_proj_ssm")
    xf, xb = _layer_norm(y_out, ln_g, ln_b, layer, tm=256)
    gp = state_re.shape[1:]
    return xf, xb, h_re.reshape((n,) + gp), h_im.reshape((n,) + gp), conv_new


def _fox_layer(xf, xb, seq_len, mem_k, mem_v, w_in, w_tail, b_f128, w_out, ln_g, ln_b, i, layer, alpha, nh, nh_mem,
               *, past):
    m, d = xf.shape
    n = m // seq_len
    fw, mw = w_out.shape[1] - mem_k.shape[2], mem_k.shape[2]
    hd = fw // nh
    zq, zk, zv, zg = (_matmul([(xb, 0, w_in, i, 0, o * fw, d)], fw, tm=1024, tn=512, name="in_proj_fox")
                      for o in range(4))
    z_tail = _matmul([(xb, 0, w_tail, i, 0, 0, d)], w_tail.shape[2], tm=1024, tn=384, name="in_proj_fox_tail")
    o_qm, o_gm, o_f = 0, mw, 2 * mw
    if past is None:
        lf, c, ct = _forget_gates(z_tail, o_f, b_f128, i, n, seq_len, tk=512)
        lf = lf[:, :, :nh]
        mix_f = _fox_attend(zq, zk, zv, zg, c, ct, n, seq_len, nh, hd, tq=512, tk=ct.shape[3])
        mix_dtype = BF16
    else:
        kc, vc, lfc_t, page_table = past
        ps = kc.shape[2]
        lf = _forget_gates_step(z_tail, o_f, b_f128, i)[:, :nh].reshape(n, seq_len, nh)
        lf_t = jnp.pad(lf.transpose(0, 2, 1), ((0, 0), (0, 0), (0, ps - seq_len)))
        pad_keys = lambda a: jnp.pad(a.reshape(n, seq_len, fw), ((0, 0), (0, ps - seq_len), (0, 0)))
        y = _fox_attend_step(zq.reshape(n, seq_len, fw), pad_keys(zk), pad_keys(zv), lf_t, kc, vc, lfc_t,
                             page_table, i, nh, hd).reshape(m, fw)
        mix_f = _gate_step(y, zg)
        mix_dtype = F32
    mix_m = _mem_attend(z_tail, o_qm, o_gm, mem_k, mem_v, layer, seq_len, nh_mem, tq=512, joint=past is not None,
                        out_dtype=mix_dtype)
    y_out = _matmul([(mix_f, 0, w_out, i, 0, 0, fw), (mix_m, 0, w_out, i, fw // mw, 0, mw)],
                    d, tm=1024, tn=256, res=xf, alpha=alpha, name="out_proj_fox")
    xf, xb = _layer_norm(y_out, ln_g, ln_b, layer, tm=256)
    shp = (n, seq_len, nh, hd)
    return xf, xb, zk.reshape(shp), zv.reshape(shp), lf


def _gate_step_body(y_ref, g_ref, o_ref):
    o_ref[...] = y_ref[...] * _silu(g_ref[...])


def _gate_step(y, g):
    m, w = y.shape
    blk = pl.BlockSpec((m, w), lambda i: (0, 0))
    return pl.pallas_call(_gate_step_body, grid=(1,), in_specs=[blk, blk], out_specs=blk,
                          out_shape=jax.ShapeDtypeStruct((m, w), F32), compiler_params=_params("arbitrary"),
                          name="gate_step")(y, g)


def kernel(x_prompt, x_sample, cache_fox_k, cache_fox_v, cache_fox_lf, cache_mem_k, cache_mem_v, state_ssm_re, state_ssm_im, state_conv, page_table, mem_prompt, w_in_ssm, w_out_ssm, ssm_lam_re, ssm_lam_im, ssm_log_dt, ssm_b_re, ssm_b_im, ssm_c_re, ssm_c_im, ssm_d, glu_w, glu_b, conv_w, conv_b, w_in_fox, b_fgate, w_out_fox, w_mem_k, w_mem_v, ln_g, ln_b):
    n_p, t_p, d = x_prompt.shape
    n_s, t_s, _ = x_sample.shape
    depth = ln_g.shape[0]
    alpha = (2 * depth) ** 0.25
    nh_fox = b_fgate.shape[1]
    nh_mem = cache_mem_k.shape[3]
    mw = w_mem_k.shape[2]
    mem_len = mem_prompt.shape[1]
    fw = w_out_fox.shape[1] - mw
    n_fox, n_pool, ps = cache_fox_k.shape[:3]

    w_tail = jnp.concatenate([w_in_fox[:, :, 4 * fw + nh_fox:], w_in_fox[:, :, 4 * fw:4 * fw + nh_fox]], axis=2)
    w_tail = jnp.pad(w_tail, ((0, 0), (0, 0), (0, LANES - nh_fox)))
    b_f128 = jnp.pad(b_fgate.astype(F32), ((0, 0), (0, LANES - nh_fox))).reshape(n_fox, 1, LANES)
    kc = cache_fox_k.reshape(n_fox, n_pool, ps, fw)
    vc = cache_fox_v.reshape(n_fox, n_pool, ps, fw)
    lfc_t = cache_fox_lf.astype(F32).transpose(0, 1, 3, 2)
    mem_k_s = cache_mem_k.reshape(depth, n_s * mem_len, mw)
    mem_v_s = cache_mem_v.reshape(depth, n_s * mem_len, mw)

    mem_in = mem_prompt.reshape(n_p * mem_len, d).astype(BF16)
    mem_k_p = _memproj(mem_in, w_mem_k)
    mem_v_p = _memproj(mem_in, w_mem_v)

    xp_f = x_prompt.reshape(n_p * t_p, d)
    xs_f = x_sample.reshape(n_s * t_s, d)
    xp_b, xs_b = xp_f.astype(BF16), xs_f.astype(BF16)
    zeros_state = jnp.zeros((n_p,) + state_ssm_re.shape[2:], F32)
    zeros_conv = jnp.zeros((n_p,) + state_conv.shape[2:], F32)

    fk_p, fv_p, flf_p, fk_s, fv_s, flf_s = [], [], [], [], [], []
    sr_p, si_p, cv_p, sr_s, si_s, cv_s = [], [], [], [], [], []
    for layer in range(depth):
        i = layer // 2
        if layer % 2 == 0:
            s5p = _s5_discretise(ssm_lam_re[i], ssm_lam_im[i], ssm_log_dt[i], ssm_b_re[i], ssm_b_im[i],
                                 ssm_c_re[i], ssm_c_im[i], ssm_d[i])
            common = (w_in_ssm, w_out_ssm, s5p, glu_w, glu_b, conv_w, conv_b, ln_g, ln_b, i, layer, alpha, nh_mem)
            xp_f, xp_b, hr, hi, cv = _ssm_conv_layer(xp_f, xp_b, t_p, zeros_state, zeros_state, zeros_conv,
                                                     mem_k_p, mem_v_p, *common, decode=False)
            sr_p.append(hr), si_p.append(hi), cv_p.append(cv)
            xs_f, xs_b, hr, hi, cv = _ssm_conv_layer(xs_f, xs_b, t_s, state_ssm_re[i].astype(F32),
                                                     state_ssm_im[i].astype(F32), state_conv[i].astype(F32),
                                                     mem_k_s, mem_v_s, *common, decode=True)
            sr_s.append(hr), si_s.append(hi), cv_s.append(cv)
        else:
            common = (w_in_fox, w_tail, b_f128, w_out_fox, ln_g, ln_b, i, layer, alpha, nh_fox, nh_mem)
            xp_f, xp_b, k, v, lf = _fox_layer(xp_f, xp_b, t_p, mem_k_p, mem_v_p, *common, past=None)
            fk_p.append(k), fv_p.append(v), flf_p.append(lf)
            xs_f, xs_b, k, v, lf = _fox_layer(xs_f, xs_b, t_s, mem_k_s, mem_v_s, *common,
                                              past=(kc, vc, lfc_t, page_table))
            fk_s.append(k), fv_s.append(v), flf_s.append(lf)
    mem_shape = (depth, n_p, mem_len, nh_mem, mw // nh_mem)
    return (xp_f.reshape(n_p, t_p, d), xs_f.reshape(n_s, t_s, d),
            jnp.stack(fk_p), jnp.stack(fv_p), jnp.stack(flf_p),
            jnp.stack(fk_s), jnp.stack(fv_s), jnp.stack(flf_s),
            jnp.stack(sr_p), jnp.stack(si_p), jnp.stack(cv_p),
            jnp.stack(sr_s), jnp.stack(si_s), jnp.stack(cv_s),
            mem_k_p.reshape(mem_shape), mem_v_p.reshape(mem_shape))
```

```python
from functools import partial

import jax
import jax.numpy as jnp
from jax import lax
from jax.experimental import pallas as pl
from jax.experimental.pallas import tpu as pltpu

F32 = jnp.float32
BF16 = jnp.bfloat16

LN_EPS = 1e-5
LANES = 128
SUBLANES = 8
VMEM_LIMIT_BYTES = 56 * 1024 * 1024
NT_DIMS = (((1,), (1,)), ((), ()))
LOG2E = 1.4426950408889634
OTHER_HEAD = -1e30


def _params(*sem):
    return pltpu.CompilerParams(dimension_semantics=sem, vmem_limit_bytes=VMEM_LIMIT_BYTES)


def _silu(g):
    return g * jax.nn.sigmoid(g)


def _mm_body(*refs, n_pairs, alpha, has_res):
    out_ref = refs[-1]
    acc = None
    for p in range(n_pairs):
        x = refs[2 * p][...].astype(BF16)
        w = refs[2 * p + 1][...].astype(BF16)
        d = jnp.dot(x, w, preferred_element_type=F32)
        acc = d if acc is None else acc + d
    if has_res:
        acc = alpha * refs[2 * n_pairs][...] + acc
    out_ref[...] = acc.astype(out_ref.dtype)


def _matmul(pairs, n_out, *, tm, tn, res=None, alpha=1.0, out_dtype=F32, name="mm"):
    m = pairs[0][0].shape[0]
    tm, tn = min(tm, m), min(tn, n_out)
    assert m % tm == 0 and n_out % tn == 0
    in_specs, args = [], []
    for (x, xkb, w, layer, wkb, col0, kdim) in pairs:
        assert col0 % tn == 0
        in_specs.append(pl.BlockSpec((tm, kdim), lambda i, j, xkb=xkb: (i, xkb)))
        in_specs.append(pl.BlockSpec((None, kdim, tn),
                                     lambda i, j, layer=layer, wkb=wkb, cb=col0 // tn: (layer, wkb, cb + j)))
        args += [x, w]
    if res is not None:
        in_specs.append(pl.BlockSpec((tm, tn), lambda i, j: (i, j)))
        args.append(res)
    return pl.pallas_call(
        partial(_mm_body, n_pairs=len(pairs), alpha=alpha, has_res=res is not None),
        grid=(m // tm, n_out // tn),
        in_specs=in_specs,
        out_specs=pl.BlockSpec((tm, tn), lambda i, j: (i, j)),
        out_shape=jax.ShapeDtypeStruct((m, n_out), out_dtype),
        compiler_params=_params("parallel", "arbitrary"),
        name=name,
    )(*args)


def _memproj_body(x_ref, w_ref, o_ref):
    o_ref[...] = jnp.dot(x_ref[...].astype(BF16), w_ref[...].astype(BF16), preferred_element_type=F32)


def _memproj(x, w):
    nl, d, n = w.shape
    r = x.shape[0]
    return pl.pallas_call(
        _memproj_body,
        grid=(nl,),
        in_specs=[pl.BlockSpec((r, d), lambda l: (0, 0)), pl.BlockSpec((None, d, n), lambda l: (l, 0, 0))],
        out_specs=pl.BlockSpec((None, r, n), lambda l: (l, 0, 0)),
        out_shape=jax.ShapeDtypeStruct((nl, r, n), F32),
        compiler_params=_params("arbitrary"),
        name="memproj",
    )(x, w)


def _ln_body(y_ref, g_ref, b_ref, xf_ref, xb_ref):
    y = y_ref[...]
    mu = jnp.mean(y, axis=-1, keepdims=True)
    d = y - mu
    var = jnp.mean(d * d, axis=-1, keepdims=True)
    out = d * lax.rsqrt(var + LN_EPS) * g_ref[...] + b_ref[...]
    xf_ref[...] = out
    xb_ref[...] = out.astype(BF16)


def _layer_norm(y, g, b, layer, *, tm):
    m, d = y.shape
    tm = min(tm, m)
    g3 = g.reshape(g.shape[0], 1, d)
    b3 = b.reshape(b.shape[0], 1, d)
    vec = pl.BlockSpec((None, 1, d), lambda i: (layer, 0, 0))
    row = pl.BlockSpec((tm, d), lambda i: (i, 0))
    return pl.pallas_call(
        _ln_body,
        grid=(m // tm,),
        in_specs=[row, vec, vec],
        out_specs=[row, row],
        out_shape=[jax.ShapeDtypeStruct((m, d), F32), jax.ShapeDtypeStruct((m, d), BF16)],
        compiler_params=_params("parallel"),
        name="layernorm",
    )(y, g3, b3)


def _s5_discretise(lam_re, lam_im, log_dt, b_re, b_im, c_re, c_im, d_skip):
    g, p, h = b_re.shape
    gpc = LANES // h
    ncc = g // gpc
    lam_re, lam_im = lam_re.astype(F32), lam_im.astype(F32)
    dt = jnp.exp(log_dt.astype(F32))[:, None]
    mag = jnp.exp(lam_re * dt)
    ar = mag * jnp.cos(lam_im * dt)
    ai = mag * jnp.sin(lam_im * dt)
    nr = ar - 1.0
    den = jnp.square(lam_re) + jnp.square(lam_im)
    cr = ((nr * lam_re + ai * lam_im) / den)[..., None]
    ci = ((ai * lam_re - nr * lam_im) / den)[..., None]
    b_re, b_im = b_re.astype(F32), b_im.astype(F32)
    bb_re = cr * b_re - ci * b_im
    bb_im = cr * b_im + ci * b_re
    eye = jnp.eye(gpc, dtype=F32)

    def pack_in(bb):
        return jnp.einsum('cgph,gk->cghkp', bb.reshape(ncc, gpc, p, h), eye).reshape(ncc, gpc * h, gpc * p)

    def pack_out(c):
        return jnp.einsum('cghp,gk->cgpkh', c.astype(F32).reshape(ncc, gpc, h, p), eye).reshape(ncc, gpc * p, gpc * h)

    return dict(a_re=ar.reshape(ncc, 1, gpc * p), a_im=ai.reshape(ncc, 1, gpc * p),
                wb_re=pack_in(bb_re), wb_im=pack_in(bb_im), wc_re=pack_out(c_re), wc_im=pack_out(c_im),
                d=d_skip.astype(F32).reshape(ncc, 1, gpc * h))


def _s5_body(u_ref, h0r_ref, h0i_ref, ar_ref, ai_ref, wbr_ref, wbi_ref, wcr_ref, wci_ref, d_ref,
             y_ref, hr_out, hi_out, hre, him, *, steps, chain):
    u = u_ref[...]
    ub = u.astype(BF16)
    hre[...] = jnp.dot(ub, wbr_ref[...].astype(BF16), preferred_element_type=F32)
    him[...] = jnp.dot(ub, wbi_ref[...].astype(BF16), preferred_element_type=F32)
    ns = hre.shape[1]
    ar = jnp.broadcast_to(ar_ref[...], (SUBLANES, ns))
    ai = jnp.broadcast_to(ai_ref[...], (SUBLANES, ns))

    def rows_of(t):
        return pl.ds(pl.multiple_of(t * SUBLANES, SUBLANES), SUBLANES)

    def scan_step(t, c):
        hr, hi = c
        rows = rows_of(t)
        nr = ar * hr - ai * hi + hre[rows, :]
        ni = ar * hi + ai * hr + him[rows, :]
        hre[rows, :] = nr
        him[rows, :] = ni
        return nr, ni

    fr, fi = lax.fori_loop(0, steps, scan_step, (h0r_ref[...], h0i_ref[...]), unroll=2)

    if chain:
        alr, ali = None, None
        sqr, sqi, e = ar, ai, steps
        while e:
            if e & 1:
                alr, ali = (sqr, sqi) if alr is None else (alr * sqr - ali * sqi, alr * sqi + ali * sqr)
            sqr, sqi = sqr * sqr - sqi * sqi, 2.0 * sqr * sqi
            e >>= 1
        sub = lax.broadcasted_iota(jnp.int32, (SUBLANES, ns), 0)
        sr, si = fr, fi
        for c in range(1, SUBLANES):
            pr, pi = pltpu.roll(sr, 1, 0), pltpu.roll(si, 1, 0)
            sr = jnp.where(sub == c, alr * pr - ali * pi + fr, sr)
            si = jnp.where(sub == c, alr * pi + ali * pr + fi, si)
        cin_r = jnp.where(sub == 0, 0.0, pltpu.roll(sr, 1, 0))
        cin_i = jnp.where(sub == 0, 0.0, pltpu.roll(si, 1, 0))

        def fix_step(t, c):
            gr, gi = c
            rows = rows_of(t)
            nr, ni = ar * gr - ai * gi, ar * gi + ai * gr
            hre[rows, :] = hre[rows, :] + nr
            him[rows, :] = him[rows, :] + ni
            return nr, ni

        lax.fori_loop(0, steps, fix_step, (cin_r, cin_i), unroll=2)
        fr, fi = sr, si

    hr_out[...] = fr
    hi_out[...] = fi
    y = (jnp.dot(hre[...].astype(BF16), wcr_ref[...].astype(BF16), preferred_element_type=F32)
         - jnp.dot(him[...].astype(BF16), wci_ref[...].astype(BF16), preferred_element_type=F32)
         + d_ref[...] * u)
    y_ref[...] = y


def _s5(u, h0_re, h0_im, prm, *, chain):
    nb, rows, w = u.shape
    steps = rows // SUBLANES
    ncc, cw, ns = prm["wb_re"].shape
    assert w == ncc * cw and cw == LANES
    seq = pl.BlockSpec((None, rows, cw), lambda b, c: (b, 0, c))
    st = pl.BlockSpec((None, SUBLANES, ns), lambda b, c: (b, 0, c))

    def par(shape):
        return pl.BlockSpec((None,) + shape, lambda b, c: (c, 0, 0))

    return pl.pallas_call(
        partial(_s5_body, steps=steps, chain=chain),
        grid=(nb, ncc),
        in_specs=[seq, st, st, par((1, ns)), par((1, ns)), par((cw, ns)), par((cw, ns)),
                  par((ns, cw)), par((ns, cw)), par((1, cw))],
        out_specs=[seq, st, st],
        out_shape=[jax.ShapeDtypeStruct((nb, rows, w), F32),
                   jax.ShapeDtypeStruct((nb, SUBLANES, ncc * ns), F32),
                   jax.ShapeDtypeStruct((nb, SUBLANES, ncc * ns), F32)],
        scratch_shapes=[pltpu.VMEM((rows, ns), F32), pltpu.VMEM((rows, ns), F32)],
        compiler_params=_params("parallel", "arbitrary"),
        name="s5_scan",
    )(u, h0_re, h0_im, prm["a_re"], prm["a_im"], prm["wb_re"], prm["wb_im"], prm["wc_re"], prm["wc_im"], prm["d"])


def _glu_body(yfull_ref, yblk_ref, ga_ref, w_ref, b_ref, o_ref, zb):
    @pl.when(pl.program_id(1) == 0)
    def _():
        zb[...] = jax.nn.gelu(yfull_ref[...]).astype(BF16)

    glu = jnp.dot(zb[...], w_ref[...].astype(BF16), preferred_element_type=F32) + b_ref[...]
    z = jax.nn.gelu(yblk_ref[...])
    o_ref[...] = (z * jax.nn.sigmoid(glu) * _silu(ga_ref[...])).astype(o_ref.dtype)


def _glu_gate(y, z_all, ga_col0, glu_w, glu_b, layer, *, tm, tn, out_dtype):
    m, w = y.shape
    tm, tn = min(tm, m), min(tn, w)
    assert ga_col0 % tn == 0
    gb3 = glu_b.reshape(glu_b.shape[0], 1, w)
    return pl.pallas_call(
        _glu_body,
        grid=(m // tm, w // tn),
        in_specs=[pl.BlockSpec((tm, w), lambda i, j: (i, 0)),
                  pl.BlockSpec((tm, tn), lambda i, j: (i, j)),
                  pl.BlockSpec((tm, tn), lambda i, j: (i, ga_col0 // tn + j)),
                  pl.BlockSpec((None, w, tn), lambda i, j: (layer, 0, j)),
                  pl.BlockSpec((None, 1, tn), lambda i, j: (layer, 0, j))],
        out_specs=pl.BlockSpec((tm, tn), lambda i, j: (i, j)),
        out_shape=jax.ShapeDtypeStruct((m, w), out_dtype),
        scratch_shapes=[pltpu.VMEM((tm, w), BF16)],
        compiler_params=_params("parallel", "arbitrary"),
        name="glu_gate",
    )(y, y, z_all, glu_w, gb3)


def _conv_body(bg_ref, cg_ref, hb_ref, gb_ref, cs_ref, w_ref, b_ref, o_ref, cn_ref, prev, *, tiles_per_seq):
    @pl.when(pl.program_id(1) % tiles_per_seq == 0)
    def _():
        prev[...] = cs_ref[...]

    v = cg_ref[...] * hb_ref[...]
    tm = v.shape[0]
    p2, p1 = prev[0:1, :], prev[1:2, :]
    row = lax.broadcasted_iota(jnp.int32, v.shape, 0)
    v1 = jnp.where(row == 0, p1, pltpu.roll(v, 1, 0))
    v2 = jnp.where(row == 0, p2, jnp.where(row == 1, p1, pltpu.roll(v, 2, 0)))
    y = b_ref[...] + w_ref[0:1, :] * v2 + w_ref[1:2, :] * v1 + w_ref[2:3, :] * v
    o_ref[...] = (bg_ref[...] * y * _silu(gb_ref[...])).astype(o_ref.dtype)
    last = v[tm - 2:tm, :]
    prev[...] = last
    cn_ref[...] = last


def _conv_gate(z_all, col0, c, conv_state, conv_w, conv_b, layer, seq_len, *, tm, ct):
    m = z_all.shape[0]
    tm, ct = min(tm, seq_len), min(ct, c)
    assert seq_len % tm == 0 and all(o % ct == 0 for o in col0) and conv_w.shape[1] == 3
    tps = seq_len // tm
    n = m // seq_len
    cb3 = conv_b.reshape(conv_b.shape[0], 1, c)

    def zcol(o):
        return pl.BlockSpec((tm, ct), lambda j, i, o=o: (i, o // ct + j))

    st = pl.BlockSpec((None, 2, ct), lambda j, i: (i // tps, 0, j))
    return pl.pallas_call(
        partial(_conv_body, tiles_per_seq=tps),
        grid=(c // ct, m // tm),
        in_specs=[zcol(col0[0]), zcol(col0[1]), zcol(col0[2]), zcol(col0[3]), st,
                  pl.BlockSpec((None, 3, ct), lambda j, i: (layer, 0, j)),
                  pl.BlockSpec((None, 1, ct), lambda j, i: (layer, 0, j))],
        out_specs=[pl.BlockSpec((tm, ct), lambda j, i: (i, j)), st],
        out_shape=[jax.ShapeDtypeStruct((m, c), BF16), jax.ShapeDtypeStruct((n, 2, c), F32)],
        scratch_shapes=[pltpu.VMEM((2, ct), F32)],
        compiler_params=_params("parallel", "arbitrary"),
        name="conv_gate",
    )(z_all, z_all, z_all, z_all, conv_state, conv_w, cb3)


def _conv_step_body(bg_ref, cg_ref, hb_ref, gb_ref, cs_ref, w_ref, b_ref, o_ref, cn_ref, *, steps):
    vp = [cs_ref[0], cs_ref[1]] + [cg_ref[t] * hb_ref[t] for t in range(steps)]
    for t in range(steps):
        y = b_ref[...] + w_ref[0:1, :] * vp[t] + w_ref[1:2, :] * vp[t + 1] + w_ref[2:3, :] * vp[t + 2]
        o_ref[t] = bg_ref[t] * y * _silu(gb_ref[t])
    cn_ref[0] = vp[steps]
    cn_ref[1] = vp[steps + 1]


def _conv_gate_step(bg, cg, hb, gb, conv_state_t, conv_w, conv_b, layer, *, ct):
    steps, n, c = bg.shape
    ct = min(ct, c)
    assert conv_w.shape[1] == 3
    cb3 = conv_b.reshape(conv_b.shape[0], 1, c)
    blk = pl.BlockSpec((steps, n, ct), lambda j: (0, 0, j))
    st = pl.BlockSpec((2, n, ct), lambda j: (0, 0, j))
    return pl.pallas_call(
        partial(_conv_step_body, steps=steps),
        grid=(c // ct,),
        in_specs=[blk, blk, blk, blk, st,
                  pl.BlockSpec((None, 3, ct), lambda j: (layer, 0, j)),
                  pl.BlockSpec((None, 1, ct), lambda j: (layer, 0, j))],
        out_specs=[blk, st],
        out_shape=[jax.ShapeDtypeStruct((steps, n, c), F32), jax.ShapeDtypeStruct((2, n, c), F32)],
        compiler_params=_params("parallel"),
        name="conv_gate_step",
    )(bg, cg, hb, gb, conv_state_t, conv_w, cb3)


def _mem_body(q_ref, g_ref, k_ref, v_ref, o_ref, *, nh, hd, rows_per_seq, keys_per_seq):
    scale = hd ** -0.5
    tq, nk = q_ref.shape[0], k_ref.shape[0]
    if rows_per_seq is not None:
        rseq = lax.broadcasted_iota(jnp.int32, (tq, nk), 0) // rows_per_seq
        kseq = lax.broadcasted_iota(jnp.int32, (tq, nk), 1) // keys_per_seq
        own = rseq == kseq
    for h in range(nh):
        sl = slice(h * hd, (h + 1) * hd)
        q = q_ref[:, sl].astype(BF16)
        k = k_ref[:, sl].astype(BF16)
        v = v_ref[:, sl].astype(BF16)
        s = lax.dot_general(q, k, NT_DIMS, preferred_element_type=F32) * scale
        if rows_per_seq is not None:
            s = jnp.where(own, s, -jnp.inf)
        e = jnp.exp(s - jnp.max(s, axis=-1, keepdims=True))
        p = e / jnp.sum(e, axis=-1, keepdims=True)
        o = jnp.dot(p.astype(BF16), v, preferred_element_type=F32)
        o_ref[:, sl] = (o * _silu(g_ref[:, sl])).astype(o_ref.dtype)


def _mem_attend(z_all, q_col0, g_col0, mem_k, mem_v, layer, seq_len, nh, *, tq, joint, out_dtype):
    m = z_all.shape[0]
    mw = mem_k.shape[2]
    n = m // seq_len
    mem_len = mem_k.shape[1] // n
    assert q_col0 % mw == 0 and g_col0 % mw == 0
    if joint:
        tq, kv_rows, rps = m, n * mem_len, seq_len
        kv_idx = lambda i: (layer, 0, 0)
    else:
        tq = min(tq, seq_len)
        assert seq_len % tq == 0
        tps = seq_len // tq
        kv_rows, rps = mem_len, None
        kv_idx = lambda i: (layer, i // tps, 0)
    kv = pl.BlockSpec((None, kv_rows, mw), kv_idx)
    return pl.pallas_call(
        partial(_mem_body, nh=nh, hd=mw // nh, rows_per_seq=rps, keys_per_seq=mem_len),
        grid=(m // tq,),
        in_specs=[pl.BlockSpec((tq, mw), lambda i: (i, q_col0 // mw)),
                  pl.BlockSpec((tq, mw), lambda i: (i, g_col0 // mw)), kv, kv],
        out_specs=pl.BlockSpec((tq, mw), lambda i: (i, 0)),
        out_shape=jax.ShapeDtypeStruct((m, mw), out_dtype),
        compiler_params=_params("parallel"),
        name="mem_attend",
    )(z_all, z_all, mem_k, mem_v)


def _log_sigmoid(x):
    return jnp.minimum(x, 0.0) - jnp.log1p(jnp.exp(-jnp.abs(x)))


def _split3(x):
    hi = x.astype(BF16)
    r = x - hi.astype(F32)
    mid = r.astype(BF16)
    return hi, mid, (r - mid.astype(F32)).astype(BF16)


def _lf_body(f_ref, bf_ref, lf_ref, c_ref, ck3_ref, r_ref, *, tk):
    lf = _log_sigmoid(f_ref[...] + bf_ref[...])
    lf_ref[...] = lf
    t = lf.shape[0]
    row = lax.broadcasted_iota(jnp.int32, lf.shape, 0)
    c, s = lf, 1
    while s < t:
        c = c + jnp.where(row >= s, pltpu.roll(c, s, 0), 0.0)
        s *= 2
    c_ref[...] = c
    for j in range(t // tk):
        rows = slice(j * tk, (j + 1) * tk)
        r = c[(j + 1) * tk - 1:(j + 1) * tk, :]
        r_ref[j:j + 1, :] = r
        for part, term in enumerate(_split3((r - c[rows, :]) * LOG2E)):
            ck3_ref[rows, part * LANES:(part + 1) * LANES] = term


def _forget_gates(z_tail, f_col0, b_f128, layer, n, seq_len, *, tk):
    assert f_col0 % LANES == 0 and seq_len % tk == 0
    z3 = z_tail.reshape(n, seq_len, z_tail.shape[1])
    blk = pl.BlockSpec((None, seq_len, LANES), lambda b: (b, 0, 0))
    return pl.pallas_call(
        partial(_lf_body, tk=tk),
        grid=(n,),
        in_specs=[pl.BlockSpec((None, seq_len, LANES), lambda b: (b, 0, f_col0 // LANES)),
                  pl.BlockSpec((None, 1, LANES), lambda b: (layer, 0, 0))],
        out_specs=[blk, blk, pl.BlockSpec((None, seq_len, 3 * LANES), lambda b: (b, 0, 0)),
                   pl.BlockSpec((None, seq_len // tk, LANES), lambda b: (b, 0, 0))],
        out_shape=[jax.ShapeDtypeStruct((n, seq_len, LANES), F32), jax.ShapeDtypeStruct((n, seq_len, LANES), F32),
                   jax.ShapeDtypeStruct((n, seq_len, 3 * LANES), BF16),
                   jax.ShapeDtypeStruct((n, seq_len // tk, LANES), F32)],
        compiler_params=_params("parallel"),
        name="forget_gates",
    )(z3, b_f128)


def _lf_step_body(f_ref, bf_ref, lf_ref, c_ref, *, steps):
    lf = _log_sigmoid(f_ref[...] + bf_ref[...])
    lf_ref[...] = lf
    step = lax.broadcasted_iota(jnp.int32, lf.shape, 0) % steps
    c, s = lf, 1
    while s < steps:
        c = c + jnp.where(step >= s, pltpu.roll(c, s, 0), 0.0)
        s *= 2
    c_ref[...] = c


def _forget_gates_step(z_tail, f_col0, b_f128, layer, steps):
    m = z_tail.shape[0]
    out = pl.BlockSpec((m, LANES), lambda i: (0, 0))
    return pl.pallas_call(
        partial(_lf_step_body, steps=steps),
        grid=(1,),
        in_specs=[pl.BlockSpec((m, LANES), lambda i: (0, f_col0 // LANES)),
                  pl.BlockSpec((None, 1, LANES), lambda i: (layer, 0, 0))],
        out_specs=[out, out],
        out_shape=[jax.ShapeDtypeStruct((m, LANES), F32), jax.ShapeDtypeStruct((m, LANES), F32)],
        compiler_params=_params("arbitrary"),
        name="forget_gates_step",
    )(z_tail, b_f128)


def _fox_body(q_ref, k_ref, v_ref, g_ref, c_ref, ck3_ref, r_ref, o_ref, ka, va, m_s, acc, *, tq, tk, hd, hps):
    h0 = pl.program_id(1) * hps
    qi = pl.program_id(2)
    seq_len = k_ref.shape[0]
    cols = lambda u: slice(u * hd, (u + 1) * hd)

    @pl.when(qi == 0)
    def _():
        sel_r = lax.broadcasted_iota(jnp.int32, (3 * LANES, hd), 0)
        sel_c = lax.broadcasted_iota(jnp.int32, (3 * LANES, hd), 1)
        one0 = jnp.where(lax.broadcasted_iota(jnp.int32, (tq, hd), 1) == 0, 1.0, 0.0).astype(BF16)
        for u in range(hps):
            sel = ((sel_r % LANES == h0 + u) & (sel_r // LANES == sel_c)).astype(BF16)
            for r0 in range(0, seq_len, tq):
                rows = slice(r0, r0 + tq)
                ka[u, rows, :hd] = k_ref[rows, cols(u)].astype(BF16)
                ka[u, rows, hd:] = jnp.dot(ck3_ref[rows, :], sel, preferred_element_type=F32).astype(BF16)
                va[u, rows, :hd] = v_ref[rows, cols(u)].astype(BF16)
                va[u, rows, hd:] = one0

    lane = lax.broadcasted_iota(jnp.int32, c_ref.shape, 1)
    lane1 = lax.broadcasted_iota(jnp.int32, (1, LANES), 1)
    ones3 = jnp.where(lax.broadcasted_iota(jnp.int32, (tq, hd), 1) < 3, 1.0, 0.0).astype(BF16)
    cq, qa = [], []
    for u in range(hps):
        cq.append(jnp.broadcast_to(jnp.sum(jnp.where(lane == h0 + u, c_ref[...], 0.0), axis=1, keepdims=True),
                                   (tq, LANES)))
        qa.append(jnp.concatenate([(q_ref[:, cols(u)] * (hd ** -0.5 * LOG2E)).astype(BF16), ones3], axis=1))
    m_s[...] = jnp.full(m_s.shape, -jnp.inf, F32)
    acc[...] = jnp.zeros(acc.shape, F32)
    reps = tk // LANES

    def kv_step(j, masked):
        ks = pl.ds(pl.multiple_of(j * tk, tk), tk)
        rj_all = r_ref[pl.ds(j, 1), :]
        for u in range(hps):
            s = lax.dot_general(qa[u], ka[u, ks, :], NT_DIMS, preferred_element_type=F32)
            if masked:
                key_minus_query = (lax.broadcasted_iota(jnp.int32, (tq, tk), 1)
                                   - lax.broadcasted_iota(jnp.int32, (tq, tk), 0))
                s = jnp.where(key_minus_query <= qi * tq - j * tk, s, -jnp.inf)
            rj = jnp.sum(jnp.where(lane1 == h0 + u, rj_all, 0.0), axis=1, keepdims=True)
            shift = (cq[u] - rj) * LOG2E
            m_old = m_s[u]
            m_new = jnp.maximum(m_old, jnp.broadcast_to(jnp.max(s, axis=1, keepdims=True), (tq, LANES)) + shift)
            p = jnp.exp2(s - jnp.concatenate([m_new - shift] * reps, axis=1))
            a = jnp.exp2(m_old - m_new)
            acc[u] = (jnp.concatenate([a, a], axis=1) * acc[u]
                      + jnp.dot(p.astype(BF16), va[u, ks, :], preferred_element_type=F32))
            m_s[u] = m_new

    n_full = (qi * tq) // tk
    lax.fori_loop(0, n_full, lambda j, c: (kv_step(j, False), c)[1], 0)
    for d in range(tq // tk):
        kv_step(n_full + d, True)
    for u in range(hps):
        o = acc[u]
        o_ref[:, cols(u)] = (o[:, :hd] / jnp.broadcast_to(o[:, hd:hd + 1], (tq, hd))
                             * _silu(g_ref[:, cols(u)])).astype(o_ref.dtype)


def _fox_attend(zq, zk, zv, zg, c, ck3, r, n, seq_len, nh, hd, *, tq, tk, hps):
    assert seq_len % tq == 0 and tq % tk == 0 and r.shape[1] == seq_len // tk and hd == LANES and nh % hps == 0
    r3 = lambda a: a.reshape(n, seq_len, nh * hd)
    qblk = pl.BlockSpec((None, tq, hps * hd), lambda b, h, i: (b, i, h))
    kvblk = pl.BlockSpec((None, seq_len, hps * hd), lambda b, h, i: (b, 0, h))
    return pl.pallas_call(
        partial(_fox_body, tq=tq, tk=tk, hd=hd, hps=hps),
        grid=(n, nh // hps, seq_len // tq),
        in_specs=[qblk, kvblk, kvblk, qblk,
                  pl.BlockSpec((None, tq, LANES), lambda b, h, i: (b, i, 0)),
                  pl.BlockSpec((None, seq_len, 3 * LANES), lambda b, h, i: (b, 0, 0)),
                  pl.BlockSpec((None, seq_len // tk, LANES), lambda b, h, i: (b, 0, 0))],
        out_specs=qblk,
        out_shape=jax.ShapeDtypeStruct((n, seq_len, nh * hd), BF16),
        scratch_shapes=[pltpu.VMEM((hps, seq_len, 2 * hd), BF16), pltpu.VMEM((hps, seq_len, 2 * hd), BF16),
                        pltpu.VMEM((hps, tq, LANES), F32), pltpu.VMEM((hps, tq, 2 * hd), F32)],
        compiler_params=_params("parallel", "parallel", "arbitrary"),
        name="fox_attend",
    )(r3(zq), r3(zk), r3(zv), r3(zg), c, ck3, r).reshape(n * seq_len, nh * hd)


def _page_sums_body(lf_ref, after_ref, tot_ref, *, stride):
    lf = lf_ref[...]
    keys = lf.shape[1]
    lane = lax.broadcasted_iota(jnp.int32, lf.shape, 1)
    sfx, s = lf, stride
    while s < keys:
        sfx = sfx + jnp.where(lane + s < keys, pltpu.roll(sfx, keys - s, 1), 0.0)
        s *= 2
    after_ref[...] = sfx - lf
    tot, s = jnp.where(lane < stride, sfx, 0.0), stride
    while s < keys:
        tot = tot + pltpu.roll(tot, s, 1)
        s *= 2
    tot_ref[...] = tot


def _page_sums(lfc, *, rows):
    nl, n_rows, keys = lfc.shape
    rows = min(rows, n_rows)
    assert n_rows % rows == 0
    blk = pl.BlockSpec((None, rows, keys), lambda l, i: (l, i, 0))
    out = jax.ShapeDtypeStruct(lfc.shape, F32)
    return pl.pallas_call(
        partial(_page_sums_body, stride=SUBLANES),
        grid=(nl, n_rows // rows),
        in_specs=[blk], out_specs=[blk, blk], out_shape=[out, out],
        compiler_params=_params("parallel", "parallel"),
        name="page_forget_sums",
    )(lfc)


def _fox_step_body(pt_ref, q_ref, kn_ref, vn_ref, cn_ref, kc_ref, vc_ref, after_ref, tot_ref, o_ref,
                   qa, ka, m_s, l_s, acc, run, *, steps, hd, n_pages):
    p = pl.program_id(1)
    n_groups = q_ref.shape[0]
    rows = steps * SUBLANES
    keys = kc_ref.shape[0] * SUBLANES

    def head_cols(n_rows, own, other):
        r = lax.broadcasted_iota(jnp.int32, (n_rows, hd), 0) % SUBLANES
        c = lax.broadcasted_iota(jnp.int32, (n_rows, hd), 1)
        return jnp.where(c == r, own, jnp.where(c < SUBLANES, other, 0.0)).astype(BF16)

    def attend(k_aug, v_b, bias):
        groups = range(n_groups)
        s = [lax.dot_general(qa[g], k_aug[g], NT_DIMS, preferred_element_type=F32) + bias[g] for g in groups]
        m_old = [m_s[g] for g in groups]
        m_new = [jnp.maximum(m_old[g], jnp.max(s[g], axis=1, keepdims=True)) for g in groups]
        e = [jnp.exp2(s[g] - m_new[g]) for g in groups]
        a = [jnp.exp2(m_old[g] - m_new[g]) for g in groups]
        pv = [jnp.dot(e[g].astype(BF16), v_b[g], preferred_element_type=F32) for g in groups]
        for g in groups:
            l_s[g] = a[g] * l_s[g] + jnp.sum(e[g], axis=1, keepdims=True)
            acc[g] = a[g] * acc[g] + pv[g]
            m_s[g] = m_new[g]

    @pl.when(p == 0)
    def _():
        m_s[...] = jnp.full(m_s.shape, -jnp.inf, F32)
        l_s[...] = jnp.zeros(l_s.shape, F32)
        acc[...] = jnp.zeros(acc.shape, F32)
        run[...] = jnp.zeros(run.shape, F32)
        key_step = lax.broadcasted_iota(jnp.int32, (rows, rows), 1) // SUBLANES
        q_step = lax.broadcasted_iota(jnp.int32, (rows, rows), 0) // SUBLANES
        causal = jnp.where(key_step <= q_step, 0.0, -jnp.inf)
        for g in range(n_groups):
            qa[g, :, :hd] = (q_ref[g] * (hd ** -0.5 * LOG2E)).astype(BF16)
            qa[g, :, hd:] = head_cols(rows, 1.0, 0.0)
            ka[g, :, hd:] = head_cols(keys, 0.0, OTHER_HEAD)
        other = head_cols(rows, 0.0, OTHER_HEAD)
        attend([jnp.concatenate([kn_ref[g].astype(BF16), other], axis=1) for g in range(n_groups)],
               [vn_ref[g].astype(BF16) for g in range(n_groups)],
               [causal - LOG2E * cn_ref[g] for g in range(n_groups)])

    bias = LOG2E * (after_ref[...] + run[...])
    run[...] = run[...] + tot_ref[...]
    for g in range(n_groups):
        ka[g, :, :hd] = kc_ref[:, g].reshape(keys, hd).astype(BF16)
    attend([ka[g] for g in range(n_groups)],
           [vc_ref[:, g].reshape(keys, hd).astype(BF16) for g in range(n_groups)],
           [bias[g:g + 1, :] for g in range(n_groups)])

    @pl.when(p == n_pages - 1)
    def _():
        o_ref[...] = acc[...] / l_s[...]


def _fox_attend_step(q, k_new, v_new, c_new, kc, vc, after, tot, page_table, layer, steps):
    n, n_groups, rows, hd = q.shape
    ps = kc.shape[2]
    keys = ps * SUBLANES
    n_pages = page_table.shape[1]
    assert rows == steps * SUBLANES and kc.shape[3:] == (n_groups, SUBLANES, hd) and keys % LANES == 0
    per_seq = lambda shape: pl.BlockSpec((None,) + shape, lambda b, p, pt: (b,) + (0,) * len(shape))
    page = lambda shape: pl.BlockSpec((None, None) + shape,
                                      lambda b, p, pt: (layer, pt[b * n_pages + n_pages - 1 - p]) + (0,) * len(shape))
    grid_spec = pltpu.PrefetchScalarGridSpec(
        num_scalar_prefetch=1,
        grid=(n, n_pages),
        in_specs=[per_seq((n_groups, rows, hd)), per_seq((n_groups, rows, hd)), per_seq((n_groups, rows, hd)),
                  per_seq((n_groups, 1, rows)),
                  page((ps, n_groups, SUBLANES, hd)), page((ps, n_groups, SUBLANES, hd)),
                  page((n_groups, keys)), page((n_groups, keys))],
        out_specs=per_seq((n_groups, rows, hd)),
        scratch_shapes=[pltpu.VMEM((n_groups, rows, 2 * hd), BF16), pltpu.VMEM((n_groups, keys, 2 * hd), BF16),
                        pltpu.VMEM((n_groups, rows, 1), F32), pltpu.VMEM((n_groups, rows, 1), F32),
                        pltpu.VMEM((n_groups, rows, hd), F32), pltpu.VMEM((n_groups, keys), F32)],
    )
    return pl.pallas_call(
        partial(_fox_step_body, steps=steps, hd=hd, n_pages=n_pages),
        grid_spec=grid_spec,
        out_shape=jax.ShapeDtypeStruct((n, n_groups, rows, hd), F32),
        compiler_params=_params("arbitrary", "arbitrary"),
        name="fox_attend_step",
    )(page_table.reshape(-1), q, k_new, v_new, c_new, kc, vc, after, tot)


def _offsets(sizes):
    out, o = [], 0
    for s in sizes:
        out.append(o)
        o += s
    return out, o


def _ssm_conv_layer(xf, xb, seq_len, state_re, state_im, conv_state, mem_k, mem_v, w_in, w_out, s5p, glu_w, glu_b,
                    conv_w, conv_b, ln_g, ln_b, i, layer, alpha, nh_mem, *, decode):
    m, d = xf.shape
    n = m // seq_len
    w, c, mw = glu_w.shape[1], conv_w.shape[2], mem_k.shape[2]
    (o_ua, o_ga, o_bg, o_cg, o_hb, o_gb, o_qm, o_gm), n_cols = _offsets((w, w, c, c, c, c, mw, mw))
    z = _matmul([(xb, 0, w_in, i, 0, 0, d)], n_cols, tm=1024, tn=512, name="in_proj_ssm")
    sps = seq_len if decode else seq_len // SUBLANES
    if decode:
        to_scan = lambda a: a.reshape(n, seq_len, -1).transpose(1, 0, 2)
        u = to_scan(z[:, o_ua:o_ua + w]).reshape(1, m, w)
        h0_re, h0_im = state_re.reshape(1, n, -1), state_im.reshape(1, n, -1)
    else:
        u = z[:, o_ua:o_ua + w].reshape(n, SUBLANES, sps, w).transpose(0, 2, 1, 3).reshape(n, seq_len, w)
        pad = lambda s: jnp.pad(s.reshape(n, 1, -1), ((0, 0), (0, SUBLANES - 1), (0, 0)))
        h0_re, h0_im = pad(state_re), pad(state_im)
    y, h_re, h_im = _s5(u, h0_re, h0_im, s5p, chain=not decode)
    if decode:
        y = y.reshape(seq_len, n, w).transpose(1, 0, 2).reshape(m, w)
        h_re, h_im = h_re[0], h_im[0]
    else:
        y = y.reshape(n, sps, SUBLANES, w).transpose(0, 2, 1, 3).reshape(m, w)
        h_re, h_im = h_re[:, SUBLANES - 1], h_im[:, SUBLANES - 1]
    mix_dtype = F32 if decode else BF16
    mix_a = _glu_gate(y, z, o_ga, glu_w, glu_b, i, tm=512, tn=512, out_dtype=mix_dtype)
    if decode:
        bg, cg, hb, gb = (to_scan(z[:, o:o + c]) for o in (o_bg, o_cg, o_hb, o_gb))
        mix_b, conv_new = _conv_gate_step(bg, cg, hb, gb, conv_state.transpose(1, 0, 2), conv_w, conv_b, i, ct=512)
        mix_b = mix_b.transpose(1, 0, 2).reshape(m, c)
        conv_new = conv_new.transpose(1, 0, 2)
    else:
        mix_b, conv_new = _conv_gate(z, (o_bg, o_cg, o_hb, o_gb), c, conv_state, conv_w, conv_b, i, seq_len,
                                     tm=512, ct=512)
    mix_m = _mem_attend(z, o_qm, o_gm, mem_k, mem_v, layer, seq_len, nh_mem, tq=512, joint=decode,
                        out_dtype=mix_dtype)
    y_out = _matmul([(mix_a, 0, w_out, i, 0, 0, w), (mix_b, 0, w_out, i, w // c, 0, c),
                     (mix_m, 0, w_out, i, (w + c) // mw, 0, mw)],
                    d, tm=1024, tn=256, res=xf, alpha=alpha, name="out_proj_ssm")
    xf, xb = _layer_norm(y_out, ln_g, ln_b, layer, tm=256)
    gp = state_re.shape[1:]
    return xf, xb, h_re.reshape((n,) + gp), h_im.reshape((n,) + gp), conv_new


def _fox_layer(xf, xb, seq_len, mem_k, mem_v, w_in, w_tail, b_f128, w_out, ln_g, ln_b, i, layer, alpha, nh, nh_mem,
               *, past):
    m, d = xf.shape
    n = m // seq_len
    fw, mw = w_out.shape[1] - mem_k.shape[2], mem_k.shape[2]
    hd = fw // nh
    zq, zk, zv, zg = (_matmul([(xb, 0, w_in, i, 0, o * fw, d)], fw, tm=1024, tn=512, name="in_proj_fox")
                      for o in range(4))
    z_tail = _matmul([(xb, 0, w_tail, i, 0, 0, d)], w_tail.shape[2], tm=1024, tn=384, name="in_proj_fox_tail")
    o_qm, o_gm, o_f = 0, mw, 2 * mw
    if past is None:
        tq = tk = min(512, seq_len)
        lf, c, ck3, r = _forget_gates(z_tail, o_f, b_f128, i, n, seq_len, tk=tk)
        lf = lf[:, :, :nh]
        mix_f = _fox_attend(zq, zk, zv, zg, c, ck3, r, n, seq_len, nh, hd, tq=tq, tk=tk, hps=2)
        mix_dtype = BF16
    else:
        kc, vc, after, tot, page_table = past
        ng = nh // SUBLANES
        lf, c_new = _forget_gates_step(z_tail, o_f, b_f128, i, seq_len)
        lf = lf[:, :nh].reshape(n, seq_len, nh)
        grouped = lambda a: (a.reshape(n, seq_len, ng, SUBLANES, -1).transpose(0, 2, 1, 3, 4)
                             .reshape(n, ng, seq_len * SUBLANES, -1))
        c_new = grouped(c_new[:, :nh]).reshape(n, ng, 1, seq_len * SUBLANES)
        y = _fox_attend_step(grouped(zq), grouped(zk), grouped(zv), c_new, kc, vc, after, tot, page_table, i,
                             seq_len)
        y = y.reshape(n, ng, seq_len, SUBLANES, hd).transpose(0, 2, 1, 3, 4).reshape(m, fw)
        mix_f = _gate_step(y, zg)
        mix_dtype = F32
    mix_m = _mem_attend(z_tail, o_qm, o_gm, mem_k, mem_v, layer, seq_len, nh_mem, tq=512, joint=past is not None,
                        out_dtype=mix_dtype)
    y_out = _matmul([(mix_f, 0, w_out, i, 0, 0, fw), (mix_m, 0, w_out, i, fw // mw, 0, mw)],
                    d, tm=1024, tn=256, res=xf, alpha=alpha, name="out_proj_fox")
    xf, xb = _layer_norm(y_out, ln_g, ln_b, layer, tm=256)
    shp = (n, seq_len, nh, hd)
    return xf, xb, zk.reshape(shp), zv.reshape(shp), lf


def _gate_step_body(y_ref, g_ref, o_ref):
    o_ref[...] = y_ref[...] * _silu(g_ref[...])


def _gate_step(y, g):
    m, w = y.shape
    blk = pl.BlockSpec((m, w), lambda i: (0, 0))
    return pl.pallas_call(_gate_step_body, grid=(1,), in_specs=[blk, blk], out_specs=blk,
                          out_shape=jax.ShapeDtypeStruct((m, w), F32), compiler_params=_params("arbitrary"),
                          name="gate_step")(y, g)


def kernel(x_prompt, x_sample, cache_fox_k, cache_fox_v, cache_fox_lf, cache_mem_k, cache_mem_v, state_ssm_re, state_ssm_im, state_conv, page_table, mem_prompt, w_in_ssm, w_out_ssm, ssm_lam_re, ssm_lam_im, ssm_log_dt, ssm_b_re, ssm_b_im, ssm_c_re, ssm_c_im, ssm_d, glu_w, glu_b, conv_w, conv_b, w_in_fox, b_fgate, w_out_fox, w_mem_k, w_mem_v, ln_g, ln_b):
    n_p, t_p, d = x_prompt.shape
    n_s, t_s, _ = x_sample.shape
    depth = ln_g.shape[0]
    alpha = (2 * depth) ** 0.25
    nh_fox = b_fgate.shape[1]
    nh_mem = cache_mem_k.shape[3]
    mw = w_mem_k.shape[2]
    mem_len = mem_prompt.shape[1]
    fw = w_out_fox.shape[1] - mw
    n_fox, n_pool, ps = cache_fox_k.shape[:3]

    w_tail = jnp.concatenate([w_in_fox[:, :, 4 * fw + nh_fox:], w_in_fox[:, :, 4 * fw:4 * fw + nh_fox]], axis=2)
    w_tail = jnp.pad(w_tail, ((0, 0), (0, 0), (0, LANES - nh_fox)))
    b_f128 = jnp.pad(b_fgate.astype(F32), ((0, 0), (0, LANES - nh_fox))).reshape(n_fox, 1, LANES)
    assert nh_fox % SUBLANES == 0
    ng = nh_fox // SUBLANES
    kc = cache_fox_k.reshape(n_fox, n_pool, ps, ng, SUBLANES, fw // nh_fox)
    vc = cache_fox_v.reshape(n_fox, n_pool, ps, ng, SUBLANES, fw // nh_fox)
    lfc = (cache_fox_lf.astype(F32).reshape(n_fox, n_pool, ps, ng, SUBLANES).transpose(0, 1, 3, 2, 4)
           .reshape(n_fox, n_pool * ng, ps * SUBLANES))
    lf_after, lf_tot = (a.reshape(n_fox, n_pool, ng, ps * SUBLANES) for a in _page_sums(lfc, rows=256))
    mem_k_s = cache_mem_k.reshape(depth, n_s * mem_len, mw)
    mem_v_s = cache_mem_v.reshape(depth, n_s * mem_len, mw)

    mem_in = mem_prompt.reshape(n_p * mem_len, d).astype(BF16)
    mem_k_p = _memproj(mem_in, w_mem_k)
    mem_v_p = _memproj(mem_in, w_mem_v)

    xp_f = x_prompt.reshape(n_p * t_p, d)
    xs_f = x_sample.reshape(n_s * t_s, d)
    xp_b, xs_b = xp_f.astype(BF16), xs_f.astype(BF16)
    zeros_state = jnp.zeros((n_p,) + state_ssm_re.shape[2:], F32)
    zeros_conv = jnp.zeros((n_p,) + state_conv.shape[2:], F32)

    fk_p, fv_p, flf_p, fk_s, fv_s, flf_s = [], [], [], [], [], []
    sr_p, si_p, cv_p, sr_s, si_s, cv_s = [], [], [], [], [], []
    for layer in range(depth):
        i = layer // 2
        if layer % 2 == 0:
            s5p = _s5_discretise(ssm_lam_re[i], ssm_lam_im[i], ssm_log_dt[i], ssm_b_re[i], ssm_b_im[i],
                                 ssm_c_re[i], ssm_c_im[i], ssm_d[i])
            common = (w_in_ssm, w_out_ssm, s5p, glu_w, glu_b, conv_w, conv_b, ln_g, ln_b, i, layer, alpha, nh_mem)
            xp_f, xp_b, hr, hi, cv = _ssm_conv_layer(xp_f, xp_b, t_p, zeros_state, zeros_state, zeros_conv,
                                                     mem_k_p, mem_v_p, *common, decode=False)
            sr_p.append(hr), si_p.append(hi), cv_p.append(cv)
            xs_f, xs_b, hr, hi, cv = _ssm_conv_layer(xs_f, xs_b, t_s, state_ssm_re[i].astype(F32),
                                                     state_ssm_im[i].astype(F32), state_conv[i].astype(F32),
                                                     mem_k_s, mem_v_s, *common, decode=True)
            sr_s.append(hr), si_s.append(hi), cv_s.append(cv)
        else:
            common = (w_in_fox, w_tail, b_f128, w_out_fox, ln_g, ln_b, i, layer, alpha, nh_fox, nh_mem)
            xp_f, xp_b, k, v, lf = _fox_layer(xp_f, xp_b, t_p, mem_k_p, mem_v_p, *common, past=None)
            fk_p.append(k), fv_p.append(v), flf_p.append(lf)
            xs_f, xs_b, k, v, lf = _fox_layer(xs_f, xs_b, t_s, mem_k_s, mem_v_s, *common,
                                              past=(kc, vc, lf_after, lf_tot, page_table))
            fk_s.append(k), fv_s.append(v), flf_s.append(lf)
    mem_shape = (depth, n_p, mem_len, nh_mem, mw // nh_mem)
    return (xp_f.reshape(n_p, t_p, d), xs_f.reshape(n_s, t_s, d),
            jnp.stack(fk_p), jnp.stack(fv_p), jnp.stack(flf_p),
            jnp.stack(fk_s), jnp.stack(fv_s), jnp.stack(flf_s),
            jnp.stack(sr_p), jnp.stack(si_p), jnp.stack(cv_p),
            jnp.stack(sr_s), jnp.stack(si_s), jnp.stack(cv_s),
            mem_k_p.reshape(mem_shape), mem_v_p.reshape(mem_shape))
```

```python
from functools import partial

import jax
import jax.numpy as jnp
from jax import lax
from jax.experimental import pallas as pl
from jax.experimental.pallas import tpu as pltpu

F32 = jnp.float32
BF16 = jnp.bfloat16

LN_EPS = 1e-5
LANES = 128
SUBLANES = 8
VMEM_LIMIT_BYTES = 56 * 1024 * 1024
NT_DIMS = (((1,), (1,)), ((), ()))
LOG2E = 1.4426950408889634


def _params(*sem):
    return pltpu.CompilerParams(dimension_semantics=sem, vmem_limit_bytes=VMEM_LIMIT_BYTES)


def _silu(g):
    return g * jax.nn.sigmoid(g)


def _mm_body(*refs, n_pairs, transposed, alpha, has_res):
    n_in = 3 * n_pairs + (2 if has_res else 0)
    out_ref, side_ref = refs[n_in], refs[n_in + 1]
    wb = refs[n_in + 2:]

    def product(x_of_pair, res_ref):
        acc = None
        for p in range(n_pairs):
            x = x_of_pair(p)[...].astype(BF16)
            if transposed:
                d = lax.dot_general(x, wb[p][...], NT_DIMS, preferred_element_type=F32)
            else:
                d = jnp.dot(x, wb[p][...], preferred_element_type=F32)
            acc = d if acc is None else acc + d
        return acc if res_ref is None else alpha * res_ref[...] + acc

    @pl.when(pl.program_id(1) == 0)
    def _():
        for p in range(n_pairs):
            wb[p][...] = refs[3 * p + 1][...].astype(BF16)
        side_ref[...] = product(lambda p: refs[3 * p + 2], refs[3 * n_pairs + 1] if has_res else None)

    out_ref[...] = product(lambda p: refs[3 * p], refs[3 * n_pairs] if has_res else None).astype(out_ref.dtype)


def _matmul(pairs, n_out, *, tm, tn, res=None, alpha=1.0, out_dtype=F32, transposed=False, name="mm"):
    m, ms = pairs[0][0].shape[0], pairs[0][1].shape[0]
    tm, tn = min(tm, m), min(tn, n_out)
    assert m % tm == 0 and n_out % tn == 0
    in_specs, args, scratch = [], [], []
    for (x, xs, xkb, w, layer, wkb, col0, kdim) in pairs:
        assert col0 % tn == 0
        cb = col0 // tn
        in_specs.append(pl.BlockSpec((tm, kdim), lambda j, i, xkb=xkb: (i, xkb)))
        if transposed:
            in_specs.append(pl.BlockSpec((None, tn, kdim), lambda j, i, layer=layer, wkb=wkb, cb=cb: (layer, cb + j, wkb)))
            scratch.append(pltpu.VMEM((tn, kdim), BF16))
        else:
            in_specs.append(pl.BlockSpec((None, kdim, tn), lambda j, i, layer=layer, wkb=wkb, cb=cb: (layer, wkb, cb + j)))
            scratch.append(pltpu.VMEM((kdim, tn), BF16))
        in_specs.append(pl.BlockSpec((ms, kdim), lambda j, i, xkb=xkb: (0, xkb)))
        args += [x, w, xs]
    if res is not None:
        in_specs += [pl.BlockSpec((tm, tn), lambda j, i: (i, j)), pl.BlockSpec((ms, tn), lambda j, i: (0, j))]
        args += list(res)
    return pl.pallas_call(
        partial(_mm_body, n_pairs=len(pairs), transposed=transposed, alpha=alpha, has_res=res is not None),
        grid=(n_out // tn, m // tm),
        in_specs=in_specs,
        out_specs=[pl.BlockSpec((tm, tn), lambda j, i: (i, j)), pl.BlockSpec((ms, tn), lambda j, i: (0, j))],
        out_shape=[jax.ShapeDtypeStruct((m, n_out), out_dtype), jax.ShapeDtypeStruct((ms, n_out), F32)],
        scratch_shapes=scratch,
        compiler_params=_params("parallel", "arbitrary"),
        name=name,
    )(*args)


def _memproj_body(x_ref, w_ref, o_ref):
    o_ref[...] = jnp.dot(x_ref[...].astype(BF16), w_ref[...].astype(BF16), preferred_element_type=F32)


def _memproj(x, w):
    nl, d, n = w.shape
    r = x.shape[0]
    return pl.pallas_call(
        _memproj_body,
        grid=(nl,),
        in_specs=[pl.BlockSpec((r, d), lambda l: (0, 0)), pl.BlockSpec((None, d, n), lambda l: (l, 0, 0))],
        out_specs=pl.BlockSpec((None, r, n), lambda l: (l, 0, 0)),
        out_shape=jax.ShapeDtypeStruct((nl, r, n), F32),
        compiler_params=_params("arbitrary"),
        name="memproj",
    )(x, w)


def _ln_body(y_ref, g_ref, b_ref, xf_ref, xb_ref):
    y = y_ref[...]
    mu = jnp.mean(y, axis=-1, keepdims=True)
    d = y - mu
    var = jnp.mean(d * d, axis=-1, keepdims=True)
    out = d * lax.rsqrt(var + LN_EPS) * g_ref[...] + b_ref[...]
    xf_ref[...] = out
    xb_ref[...] = out.astype(BF16)


def _layer_norm(y, g, b, layer, *, tm):
    m, d = y.shape
    tm = min(tm, m)
    g3 = g.reshape(g.shape[0], 1, d)
    b3 = b.reshape(b.shape[0], 1, d)
    vec = pl.BlockSpec((None, 1, d), lambda i: (layer, 0, 0))
    row = pl.BlockSpec((tm, d), lambda i: (i, 0))
    return pl.pallas_call(
        _ln_body,
        grid=(m // tm,),
        in_specs=[row, vec, vec],
        out_specs=[row, row],
        out_shape=[jax.ShapeDtypeStruct((m, d), F32), jax.ShapeDtypeStruct((m, d), BF16)],
        compiler_params=_params("parallel"),
        name="layernorm",
    )(y, g3, b3)


def _s5_discretise(lam_re, lam_im, log_dt, b_re, b_im, c_re, c_im, d_skip):
    g, p, h = b_re.shape
    gpc = LANES // h
    ncc = g // gpc
    lam_re, lam_im = lam_re.astype(F32), lam_im.astype(F32)
    dt = jnp.exp(log_dt.astype(F32))[:, None]
    mag = jnp.exp(lam_re * dt)
    ar = mag * jnp.cos(lam_im * dt)
    ai = mag * jnp.sin(lam_im * dt)
    nr = ar - 1.0
    den = jnp.square(lam_re) + jnp.square(lam_im)
    cr = ((nr * lam_re + ai * lam_im) / den)[..., None]
    ci = ((ai * lam_re - nr * lam_im) / den)[..., None]
    b_re, b_im = b_re.astype(F32), b_im.astype(F32)
    bb_re = cr * b_re - ci * b_im
    bb_im = cr * b_im + ci * b_re
    eye = jnp.eye(gpc, dtype=F32)

    def pack_in(bb):
        return jnp.einsum('cgph,gk->cghkp', bb.reshape(ncc, gpc, p, h), eye).reshape(ncc, gpc * h, gpc * p)

    def pack_out(c):
        return jnp.einsum('cghp,gk->cgpkh', c.astype(F32).reshape(ncc, gpc, h, p), eye).reshape(ncc, gpc * p, gpc * h)

    return dict(a_re=ar.reshape(ncc, 1, gpc * p), a_im=ai.reshape(ncc, 1, gpc * p),
                wb_re=pack_in(bb_re), wb_im=pack_in(bb_im), wc_re=pack_out(c_re), wc_im=pack_out(c_im),
                d=d_skip.astype(F32).reshape(ncc, 1, gpc * h))


def _s5_body(u_ref, h0r_ref, h0i_ref, ar_ref, ai_ref, wbr_ref, wbi_ref, wcr_ref, wci_ref, d_ref,
             y_ref, hr_out, hi_out, hre, him, *, steps, chain):
    u = u_ref[...]
    ub = u.astype(BF16)
    hre[...] = jnp.dot(ub, wbr_ref[...].astype(BF16), preferred_element_type=F32)
    him[...] = jnp.dot(ub, wbi_ref[...].astype(BF16), preferred_element_type=F32)
    ns = hre.shape[1]
    ar = jnp.broadcast_to(ar_ref[...], (SUBLANES, ns))
    ai = jnp.broadcast_to(ai_ref[...], (SUBLANES, ns))

    def rows_of(t):
        return pl.ds(pl.multiple_of(t * SUBLANES, SUBLANES), SUBLANES)

    def scan_step(t, c):
        hr, hi = c
        rows = rows_of(t)
        nr = ar * hr - ai * hi + hre[rows, :]
        ni = ar * hi + ai * hr + him[rows, :]
        hre[rows, :] = nr
        him[rows, :] = ni
        return nr, ni

    fr, fi = lax.fori_loop(0, steps, scan_step, (h0r_ref[...], h0i_ref[...]), unroll=2)

    if chain:
        alr, ali = None, None
        sqr, sqi, e = ar, ai, steps
        while e:
            if e & 1:
                alr, ali = (sqr, sqi) if alr is None else (alr * sqr - ali * sqi, alr * sqi + ali * sqr)
            sqr, sqi = sqr * sqr - sqi * sqi, 2.0 * sqr * sqi
            e >>= 1
        sub = lax.broadcasted_iota(jnp.int32, (SUBLANES, ns), 0)
        sr, si = fr, fi
        for c in range(1, SUBLANES):
            pr, pi = pltpu.roll(sr, 1, 0), pltpu.roll(si, 1, 0)
            sr = jnp.where(sub == c, alr * pr - ali * pi + fr, sr)
            si = jnp.where(sub == c, alr * pi + ali * pr + fi, si)
        cin_r = jnp.where(sub == 0, 0.0, pltpu.roll(sr, 1, 0))
        cin_i = jnp.where(sub == 0, 0.0, pltpu.roll(si, 1, 0))

        def fix_step(t, c):
            gr, gi = c
            rows = rows_of(t)
            nr, ni = ar * gr - ai * gi, ar * gi + ai * gr
            hre[rows, :] = hre[rows, :] + nr
            him[rows, :] = him[rows, :] + ni
            return nr, ni

        lax.fori_loop(0, steps, fix_step, (cin_r, cin_i), unroll=2)
        fr, fi = sr, si

    hr_out[...] = fr
    hi_out[...] = fi
    y = (jnp.dot(hre[...].astype(BF16), wcr_ref[...].astype(BF16), preferred_element_type=F32)
         - jnp.dot(him[...].astype(BF16), wci_ref[...].astype(BF16), preferred_element_type=F32)
         + d_ref[...] * u)
    y_ref[...] = y


def _s5(u, h0_re, h0_im, prm, *, chain):
    nb, rows, w = u.shape
    steps = rows // SUBLANES
    ncc, cw, ns = prm["wb_re"].shape
    assert w == ncc * cw and cw == LANES
    seq = pl.BlockSpec((None, rows, cw), lambda b, c: (b, 0, c))
    st = pl.BlockSpec((None, SUBLANES, ns), lambda b, c: (b, 0, c))

    def par(shape):
        return pl.BlockSpec((None,) + shape, lambda b, c: (c, 0, 0))

    return pl.pallas_call(
        partial(_s5_body, steps=steps, chain=chain),
        grid=(nb, ncc),
        in_specs=[seq, st, st, par((1, ns)), par((1, ns)), par((cw, ns)), par((cw, ns)),
                  par((ns, cw)), par((ns, cw)), par((1, cw))],
        out_specs=[seq, st, st],
        out_shape=[jax.ShapeDtypeStruct((nb, rows, w), F32),
                   jax.ShapeDtypeStruct((nb, SUBLANES, ncc * ns), F32),
                   jax.ShapeDtypeStruct((nb, SUBLANES, ncc * ns), F32)],
        scratch_shapes=[pltpu.VMEM((rows, ns), F32), pltpu.VMEM((rows, ns), F32)],
        compiler_params=_params("parallel", "arbitrary"),
        name="s5_scan",
    )(u, h0_re, h0_im, prm["a_re"], prm["a_im"], prm["wb_re"], prm["wb_im"], prm["wc_re"], prm["wc_im"], prm["d"])


def _glu_body(yfull_ref, yblk_ref, ga_ref, w_ref, b_ref, o_ref, zb):
    @pl.when(pl.program_id(1) == 0)
    def _():
        zb[...] = jax.nn.gelu(yfull_ref[...]).astype(BF16)

    glu = jnp.dot(zb[...], w_ref[...].astype(BF16), preferred_element_type=F32) + b_ref[...]
    z = jax.nn.gelu(yblk_ref[...])
    o_ref[...] = (z * jax.nn.sigmoid(glu) * _silu(ga_ref[...])).astype(o_ref.dtype)


def _glu_gate(y, z_all, ga_col0, glu_w, glu_b, layer, *, tm, tn, out_dtype):
    m, w = y.shape
    tm, tn = min(tm, m), min(tn, w)
    assert ga_col0 % tn == 0
    gb3 = glu_b.reshape(glu_b.shape[0], 1, w)
    return pl.pallas_call(
        _glu_body,
        grid=(m // tm, w // tn),
        in_specs=[pl.BlockSpec((tm, w), lambda i, j: (i, 0)),
                  pl.BlockSpec((tm, tn), lambda i, j: (i, j)),
                  pl.BlockSpec((tm, tn), lambda i, j: (i, ga_col0 // tn + j)),
                  pl.BlockSpec((None, w, tn), lambda i, j: (layer, 0, j)),
                  pl.BlockSpec((None, 1, tn), lambda i, j: (layer, 0, j))],
        out_specs=pl.BlockSpec((tm, tn), lambda i, j: (i, j)),
        out_shape=jax.ShapeDtypeStruct((m, w), out_dtype),
        scratch_shapes=[pltpu.VMEM((tm, w), BF16)],
        compiler_params=_params("parallel", "arbitrary"),
        name="glu_gate",
    )(y, y, z_all, glu_w, gb3)


def _conv_body(bg_ref, cg_ref, hb_ref, gb_ref, cs_ref, w_ref, b_ref, o_ref, cn_ref, prev, *, tiles_per_seq):
    @pl.when(pl.program_id(1) % tiles_per_seq == 0)
    def _():
        prev[...] = cs_ref[...]

    v = cg_ref[...] * hb_ref[...]
    tm = v.shape[0]
    p2, p1 = prev[0:1, :], prev[1:2, :]
    row = lax.broadcasted_iota(jnp.int32, v.shape, 0)
    v1 = jnp.where(row == 0, p1, pltpu.roll(v, 1, 0))
    v2 = jnp.where(row == 0, p2, jnp.where(row == 1, p1, pltpu.roll(v, 2, 0)))
    y = b_ref[...] + w_ref[0:1, :] * v2 + w_ref[1:2, :] * v1 + w_ref[2:3, :] * v
    o_ref[...] = (bg_ref[...] * y * _silu(gb_ref[...])).astype(o_ref.dtype)
    last = v[tm - 2:tm, :]
    prev[...] = last
    cn_ref[...] = last


def _conv_gate(z_all, col0, c, conv_state, conv_w, conv_b, layer, seq_len, *, tm, ct):
    m = z_all.shape[0]
    tm, ct = min(tm, seq_len), min(ct, c)
    assert seq_len % tm == 0 and all(o % ct == 0 for o in col0) and conv_w.shape[1] == 3
    tps = seq_len // tm
    n = m // seq_len
    cb3 = conv_b.reshape(conv_b.shape[0], 1, c)

    def zcol(o):
        return pl.BlockSpec((tm, ct), lambda j, i, o=o: (i, o // ct + j))

    st = pl.BlockSpec((None, 2, ct), lambda j, i: (i // tps, 0, j))
    return pl.pallas_call(
        partial(_conv_body, tiles_per_seq=tps),
        grid=(c // ct, m // tm),
        in_specs=[zcol(col0[0]), zcol(col0[1]), zcol(col0[2]), zcol(col0[3]), st,
                  pl.BlockSpec((None, 3, ct), lambda j, i: (layer, 0, j)),
                  pl.BlockSpec((None, 1, ct), lambda j, i: (layer, 0, j))],
        out_specs=[pl.BlockSpec((tm, ct), lambda j, i: (i, j)), st],
        out_shape=[jax.ShapeDtypeStruct((m, c), BF16), jax.ShapeDtypeStruct((n, 2, c), F32)],
        scratch_shapes=[pltpu.VMEM((2, ct), F32)],
        compiler_params=_params("parallel", "arbitrary"),
        name="conv_gate",
    )(z_all, z_all, z_all, z_all, conv_state, conv_w, cb3)


def _conv_step_body(bg_ref, cg_ref, hb_ref, gb_ref, cs_ref, w_ref, b_ref, o_ref, cn_ref, *, steps):
    vp = [cs_ref[0], cs_ref[1]] + [cg_ref[t] * hb_ref[t] for t in range(steps)]
    for t in range(steps):
        y = b_ref[...] + w_ref[0:1, :] * vp[t] + w_ref[1:2, :] * vp[t + 1] + w_ref[2:3, :] * vp[t + 2]
        o_ref[t] = bg_ref[t] * y * _silu(gb_ref[t])
    cn_ref[0] = vp[steps]
    cn_ref[1] = vp[steps + 1]


def _conv_gate_step(bg, cg, hb, gb, conv_state_t, conv_w, conv_b, layer, *, ct):
    steps, n, c = bg.shape
    ct = min(ct, c)
    assert conv_w.shape[1] == 3
    cb3 = conv_b.reshape(conv_b.shape[0], 1, c)
    blk = pl.BlockSpec((steps, n, ct), lambda j: (0, 0, j))
    st = pl.BlockSpec((2, n, ct), lambda j: (0, 0, j))
    return pl.pallas_call(
        partial(_conv_step_body, steps=steps),
        grid=(c // ct,),
        in_specs=[blk, blk, blk, blk, st,
                  pl.BlockSpec((None, 3, ct), lambda j: (layer, 0, j)),
                  pl.BlockSpec((None, 1, ct), lambda j: (layer, 0, j))],
        out_specs=[blk, st],
        out_shape=[jax.ShapeDtypeStruct((steps, n, c), F32), jax.ShapeDtypeStruct((2, n, c), F32)],
        compiler_params=_params("parallel"),
        name="conv_gate_step",
    )(bg, cg, hb, gb, conv_state_t, conv_w, cb3)


def _mem_body(q_ref, g_ref, k_ref, v_ref, o_ref, *, nh, hd, rows_per_seq, keys_per_seq):
    scale = hd ** -0.5
    tq, nk = q_ref.shape[0], k_ref.shape[0]
    if rows_per_seq is not None:
        rseq = lax.broadcasted_iota(jnp.int32, (tq, nk), 0) // rows_per_seq
        kseq = lax.broadcasted_iota(jnp.int32, (tq, nk), 1) // keys_per_seq
        own = rseq == kseq
    for h in range(nh):
        sl = slice(h * hd, (h + 1) * hd)
        q = q_ref[:, sl].astype(BF16)
        k = k_ref[:, sl].astype(BF16)
        v = v_ref[:, sl].astype(BF16)
        s = lax.dot_general(q, k, NT_DIMS, preferred_element_type=F32) * scale
        if rows_per_seq is not None:
            s = jnp.where(own, s, -jnp.inf)
        e = jnp.exp(s - jnp.max(s, axis=-1, keepdims=True))
        p = e / jnp.sum(e, axis=-1, keepdims=True)
        o = jnp.dot(p.astype(BF16), v, preferred_element_type=F32)
        o_ref[:, sl] = (o * _silu(g_ref[:, sl])).astype(o_ref.dtype)


def _mem_attend(z_all, q_col0, g_col0, mem_k, mem_v, layer, seq_len, nh, *, tq, joint, out_dtype):
    m = z_all.shape[0]
    mw = mem_k.shape[2]
    n = m // seq_len
    mem_len = mem_k.shape[1] // n
    assert q_col0 % mw == 0 and g_col0 % mw == 0
    if joint:
        tq, kv_rows, rps = m, n * mem_len, seq_len
        kv_idx = lambda i: (layer, 0, 0)
    else:
        tq = min(tq, seq_len)
        assert seq_len % tq == 0
        tps = seq_len // tq
        kv_rows, rps = mem_len, None
        kv_idx = lambda i: (layer, i // tps, 0)
    kv = pl.BlockSpec((None, kv_rows, mw), kv_idx)
    return pl.pallas_call(
        partial(_mem_body, nh=nh, hd=mw // nh, rows_per_seq=rps, keys_per_seq=mem_len),
        grid=(m // tq,),
        in_specs=[pl.BlockSpec((tq, mw), lambda i: (i, q_col0 // mw)),
                  pl.BlockSpec((tq, mw), lambda i: (i, g_col0 // mw)), kv, kv],
        out_specs=pl.BlockSpec((tq, mw), lambda i: (i, 0)),
        out_shape=jax.ShapeDtypeStruct((m, mw), out_dtype),
        compiler_params=_params("parallel"),
        name="mem_attend",
    )(z_all, z_all, mem_k, mem_v)


def _log_sigmoid(x):
    return jnp.minimum(x, 0.0) - jnp.log1p(jnp.exp(-jnp.abs(x)))


def _split3(x):
    hi = x.astype(BF16)
    r = x - hi.astype(F32)
    mid = r.astype(BF16)
    return hi, mid, (r - mid.astype(F32)).astype(BF16)


def _lf_body(f_ref, bf_ref, lf_ref, c_ref, ck3_ref, r_ref, *, tk):
    lf = _log_sigmoid(f_ref[...] + bf_ref[...])
    lf_ref[...] = lf
    t = lf.shape[0]
    row = lax.broadcasted_iota(jnp.int32, lf.shape, 0)
    c, s = lf, 1
    while s < t:
        c = c + jnp.where(row >= s, pltpu.roll(c, s, 0), 0.0)
        s *= 2
    c_ref[...] = c
    for j in range(t // tk):
        rows = slice(j * tk, (j + 1) * tk)
        r = c[(j + 1) * tk - 1:(j + 1) * tk, :]
        r_ref[j:j + 1, :] = r
        for part, term in enumerate(_split3((r - c[rows, :]) * LOG2E)):
            ck3_ref[rows, part * LANES:(part + 1) * LANES] = term


def _forget_gates(z_tail, f_col0, b_f128, layer, n, seq_len, *, tk):
    assert f_col0 % LANES == 0 and seq_len % tk == 0
    z3 = z_tail.reshape(n, seq_len, z_tail.shape[1])
    blk = pl.BlockSpec((None, seq_len, LANES), lambda b: (b, 0, 0))
    return pl.pallas_call(
        partial(_lf_body, tk=tk),
        grid=(n,),
        in_specs=[pl.BlockSpec((None, seq_len, LANES), lambda b: (b, 0, f_col0 // LANES)),
                  pl.BlockSpec((None, 1, LANES), lambda b: (layer, 0, 0))],
        out_specs=[blk, blk, pl.BlockSpec((None, seq_len, 3 * LANES), lambda b: (b, 0, 0)),
                   pl.BlockSpec((None, seq_len // tk, LANES), lambda b: (b, 0, 0))],
        out_shape=[jax.ShapeDtypeStruct((n, seq_len, LANES), F32), jax.ShapeDtypeStruct((n, seq_len, LANES), F32),
                   jax.ShapeDtypeStruct((n, seq_len, 3 * LANES), BF16),
                   jax.ShapeDtypeStruct((n, seq_len // tk, LANES), F32)],
        compiler_params=_params("parallel"),
        name="forget_gates",
    )(z3, b_f128)


def _lf_step_body(f_ref, bf_ref, lf_ref, c_ref, *, steps):
    lf = _log_sigmoid(f_ref[...] + bf_ref[...])
    lf_ref[...] = lf
    step = lax.broadcasted_iota(jnp.int32, lf.shape, 0) % steps
    c, s = lf, 1
    while s < steps:
        c = c + jnp.where(step >= s, pltpu.roll(c, s, 0), 0.0)
        s *= 2
    c_ref[...] = c


def _forget_gates_step(z_tail, f_col0, b_f128, layer, steps):
    m = z_tail.shape[0]
    out = pl.BlockSpec((m, LANES), lambda i: (0, 0))
    return pl.pallas_call(
        partial(_lf_step_body, steps=steps),
        grid=(1,),
        in_specs=[pl.BlockSpec((m, LANES), lambda i: (0, f_col0 // LANES)),
                  pl.BlockSpec((None, 1, LANES), lambda i: (layer, 0, 0))],
        out_specs=[out, out],
        out_shape=[jax.ShapeDtypeStruct((m, LANES), F32), jax.ShapeDtypeStruct((m, LANES), F32)],
        compiler_params=_params("arbitrary"),
        name="forget_gates_step",
    )(z_tail, b_f128)


def _fox_body(q_ref, k_ref, v_ref, g_ref, c_ref, ck3_ref, r_ref, o_ref, ka, va, m_s, acc, *, tq, tk, hd, hps):
    h0 = pl.program_id(1) * hps
    qi = pl.program_id(2)
    seq_len = k_ref.shape[0]
    cols = lambda u: slice(u * hd, (u + 1) * hd)

    @pl.when(qi == 0)
    def _():
        sel_r = lax.broadcasted_iota(jnp.int32, (3 * LANES, hd), 0)
        sel_c = lax.broadcasted_iota(jnp.int32, (3 * LANES, hd), 1)
        one0 = jnp.where(lax.broadcasted_iota(jnp.int32, (tq, hd), 1) == 0, 1.0, 0.0).astype(BF16)
        for u in range(hps):
            sel = ((sel_r % LANES == h0 + u) & (sel_r // LANES == sel_c)).astype(BF16)
            for r0 in range(0, seq_len, tq):
                rows = slice(r0, r0 + tq)
                ka[u, rows, :hd] = k_ref[rows, cols(u)].astype(BF16)
                ka[u, rows, hd:] = jnp.dot(ck3_ref[rows, :], sel, preferred_element_type=F32).astype(BF16)
                va[u, rows, :hd] = v_ref[rows, cols(u)].astype(BF16)
                va[u, rows, hd:] = one0

    lane = lax.broadcasted_iota(jnp.int32, c_ref.shape, 1)
    lane1 = lax.broadcasted_iota(jnp.int32, (1, LANES), 1)
    ones3 = jnp.where(lax.broadcasted_iota(jnp.int32, (tq, hd), 1) < 3, 1.0, 0.0).astype(BF16)
    cq, qa = [], []
    for u in range(hps):
        cq.append(jnp.broadcast_to(jnp.sum(jnp.where(lane == h0 + u, c_ref[...], 0.0), axis=1, keepdims=True),
                                   (tq, LANES)))
        qa.append(jnp.concatenate([(q_ref[:, cols(u)] * (hd ** -0.5 * LOG2E)).astype(BF16), ones3], axis=1))
    m_s[...] = jnp.full(m_s.shape, -jnp.inf, F32)
    acc[...] = jnp.zeros(acc.shape, F32)
    reps = tk // LANES

    def kv_step(j, masked):
        ks = pl.ds(pl.multiple_of(j * tk, tk), tk)
        rj_all = r_ref[pl.ds(j, 1), :]
        for u in range(hps):
            s = lax.dot_general(qa[u], ka[u, ks, :], NT_DIMS, preferred_element_type=F32)
            if masked:
                key_minus_query = (lax.broadcasted_iota(jnp.int32, (tq, tk), 1)
                                   - lax.broadcasted_iota(jnp.int32, (tq, tk), 0))
                s = jnp.where(key_minus_query <= qi * tq - j * tk, s, -jnp.inf)
            rj = jnp.sum(jnp.where(lane1 == h0 + u, rj_all, 0.0), axis=1, keepdims=True)
            shift = (cq[u] - rj) * LOG2E
            m_old = m_s[u]
            m_new = jnp.maximum(m_old, jnp.broadcast_to(jnp.max(s, axis=1, keepdims=True), (tq, LANES)) + shift)
            p = jnp.exp2(s - jnp.concatenate([m_new - shift] * reps, axis=1))
            a = jnp.exp2(m_old - m_new)
            acc[u] = (jnp.concatenate([a, a], axis=1) * acc[u]
                      + jnp.dot(p.astype(BF16), va[u, ks, :], preferred_element_type=F32))
            m_s[u] = m_new

    n_full = (qi * tq) // tk
    lax.fori_loop(0, n_full, lambda j, c: (kv_step(j, False), c)[1], 0)
    for d in range(tq // tk):
        kv_step(n_full + d, True)
    for u in range(hps):
        o = acc[u]
        o_ref[:, cols(u)] = (o[:, :hd] / jnp.broadcast_to(o[:, hd:hd + 1], (tq, hd))
                             * _silu(g_ref[:, cols(u)])).astype(o_ref.dtype)


def _fox_attend(zq, zk, zv, zg, c, ck3, r, n, seq_len, nh, hd, *, tq, tk, hps):
    assert seq_len % tq == 0 and tq % tk == 0 and r.shape[1] == seq_len // tk and hd == LANES and nh % hps == 0
    r3 = lambda a: a.reshape(n, seq_len, nh * hd)
    qblk = pl.BlockSpec((None, tq, hps * hd), lambda b, h, i: (b, i, h))
    kvblk = pl.BlockSpec((None, seq_len, hps * hd), lambda b, h, i: (b, 0, h))
    return pl.pallas_call(
        partial(_fox_body, tq=tq, tk=tk, hd=hd, hps=hps),
        grid=(n, nh // hps, seq_len // tq),
        in_specs=[qblk, kvblk, kvblk, qblk,
                  pl.BlockSpec((None, tq, LANES), lambda b, h, i: (b, i, 0)),
                  pl.BlockSpec((None, seq_len, 3 * LANES), lambda b, h, i: (b, 0, 0)),
                  pl.BlockSpec((None, seq_len // tk, LANES), lambda b, h, i: (b, 0, 0))],
        out_specs=qblk,
        out_shape=jax.ShapeDtypeStruct((n, seq_len, nh * hd), BF16),
        scratch_shapes=[pltpu.VMEM((hps, seq_len, 2 * hd), BF16), pltpu.VMEM((hps, seq_len, 2 * hd), BF16),
                        pltpu.VMEM((hps, tq, LANES), F32), pltpu.VMEM((hps, tq, 2 * hd), F32)],
        compiler_params=_params("parallel", "parallel", "arbitrary"),
        name="fox_attend",
    )(r3(zq), r3(zk), r3(zv), r3(zg), c, ck3, r).reshape(n * seq_len, nh * hd)


def _page_sums_body(lf_ref, after_ref, tot_ref, *, stride):
    lf = lf_ref[...]
    keys = lf.shape[1]
    lane = lax.broadcasted_iota(jnp.int32, lf.shape, 1)
    sfx, s = lf, stride
    while s < keys:
        sfx = sfx + jnp.where(lane + s < keys, pltpu.roll(sfx, keys - s, 1), 0.0)
        s *= 2
    after_ref[...] = sfx - lf
    tot, s = jnp.where(lane < stride, sfx, 0.0), stride
    while s < keys:
        tot = tot + pltpu.roll(tot, s, 1)
        s *= 2
    tot_ref[...] = tot


def _page_sums(lfc, *, rows):
    nl, n_rows, keys = lfc.shape
    rows = min(rows, n_rows)
    assert n_rows % rows == 0
    blk = pl.BlockSpec((None, rows, keys), lambda l, i: (l, i, 0))
    out = jax.ShapeDtypeStruct(lfc.shape, F32)
    return pl.pallas_call(
        partial(_page_sums_body, stride=SUBLANES),
        grid=(nl, n_rows // rows),
        in_specs=[blk], out_specs=[blk, blk], out_shape=[out, out],
        compiler_params=_params("parallel", "parallel"),
        name="page_forget_sums",
    )(lfc)


def _fox_step_body(pt_ref, q_ref, kn_ref, vn_ref, cn_ref, kc_ref, vc_ref, after_ref, tot_ref, o_ref,
                   qb, own_s, m_s, l_s, acc, run, *, steps, hd, n_pages):
    p = pl.program_id(1)
    n_groups = q_ref.shape[0]
    rows = steps * SUBLANES
    keys = kc_ref.shape[0] * SUBLANES

    def same_head(n_keys):
        r = lax.broadcasted_iota(jnp.int32, (rows, n_keys), 0) % SUBLANES
        c = lax.broadcasted_iota(jnp.int32, (rows, n_keys), 1) % SUBLANES
        return jnp.where(r == c, 0.0, -jnp.inf)

    def attend(k_b, v_b, bias):
        groups = range(n_groups)
        s = [lax.dot_general(qb[g], k_b[g], NT_DIMS, preferred_element_type=F32) + bias[g] for g in groups]
        m_old = [m_s[g] for g in groups]
        m_new = [jnp.maximum(m_old[g], jnp.max(s[g], axis=1, keepdims=True)) for g in groups]
        e = [jnp.exp2(s[g] - m_new[g]) for g in groups]
        a = [jnp.exp2(m_old[g] - m_new[g]) for g in groups]
        pv = [jnp.dot(e[g].astype(BF16), v_b[g], preferred_element_type=F32) for g in groups]
        for g in groups:
            l_s[g] = a[g] * l_s[g] + jnp.sum(e[g], axis=1, keepdims=True)
            acc[g] = a[g] * acc[g] + pv[g]
            m_s[g] = m_new[g]

    @pl.when(p == 0)
    def _():
        m_s[...] = jnp.full(m_s.shape, -jnp.inf, F32)
        l_s[...] = jnp.zeros(l_s.shape, F32)
        acc[...] = jnp.zeros(acc.shape, F32)
        run[...] = jnp.zeros(run.shape, F32)
        own_s[...] = same_head(keys)
        key_step = lax.broadcasted_iota(jnp.int32, (rows, rows), 1) // SUBLANES
        q_step = lax.broadcasted_iota(jnp.int32, (rows, rows), 0) // SUBLANES
        visible = jnp.where(key_step <= q_step, same_head(rows), -jnp.inf)
        for g in range(n_groups):
            qb[g] = (q_ref[g] * (hd ** -0.5 * LOG2E)).astype(BF16)
        attend([kn_ref[g].astype(BF16) for g in range(n_groups)],
               [vn_ref[g].astype(BF16) for g in range(n_groups)],
               [visible - LOG2E * cn_ref[g] for g in range(n_groups)])

    bias = LOG2E * (after_ref[...] + run[...])
    run[...] = run[...] + tot_ref[...]
    attend([kc_ref[:, g].reshape(keys, hd).astype(BF16) for g in range(n_groups)],
           [vc_ref[:, g].reshape(keys, hd).astype(BF16) for g in range(n_groups)],
           [own_s[...] + bias[g:g + 1, :] for g in range(n_groups)])

    @pl.when(p == n_pages - 1)
    def _():
        o_ref[...] = acc[...] / l_s[...]


def _fox_attend_step(q, k_new, v_new, c_new, kc, vc, after, tot, page_table, layer, steps):
    n, n_groups, rows, hd = q.shape
    ps = kc.shape[2]
    keys = ps * SUBLANES
    n_pages = page_table.shape[1]
    assert rows == steps * SUBLANES and kc.shape[3:] == (n_groups, SUBLANES, hd) and keys % LANES == 0
    per_seq = lambda shape: pl.BlockSpec((None,) + shape, lambda b, p, pt: (b,) + (0,) * len(shape))
    page = lambda shape: pl.BlockSpec((None, None) + shape,
                                      lambda b, p, pt: (layer, pt[b * n_pages + n_pages - 1 - p]) + (0,) * len(shape))
    grid_spec = pltpu.PrefetchScalarGridSpec(
        num_scalar_prefetch=1,
        grid=(n, n_pages),
        in_specs=[per_seq((n_groups, rows, hd)), per_seq((n_groups, rows, hd)), per_seq((n_groups, rows, hd)),
                  per_seq((n_groups, 1, rows)),
                  page((ps, n_groups, SUBLANES, hd)), page((ps, n_groups, SUBLANES, hd)),
                  page((n_groups, keys)), page((n_groups, keys))],
        out_specs=per_seq((n_groups, rows, hd)),
        scratch_shapes=[pltpu.VMEM((n_groups, rows, hd), BF16), pltpu.VMEM((rows, keys), F32),
                        pltpu.VMEM((n_groups, rows, 1), F32), pltpu.VMEM((n_groups, rows, 1), F32),
                        pltpu.VMEM((n_groups, rows, hd), F32), pltpu.VMEM((n_groups, keys), F32)],
    )
    return pl.pallas_call(
        partial(_fox_step_body, steps=steps, hd=hd, n_pages=n_pages),
        grid_spec=grid_spec,
        out_shape=jax.ShapeDtypeStruct((n, n_groups, rows, hd), F32),
        compiler_params=_params("arbitrary", "arbitrary"),
        name="fox_attend_step",
    )(page_table.reshape(-1), q, k_new, v_new, c_new, kc, vc, after, tot)


def _offsets(sizes):
    out, o = [], 0
    for s in sizes:
        out.append(o)
        o += s
    return out, o


def _ssm_conv_mix(z, seq_len, state_re, state_im, conv_state, mem_k, mem_v, s5p, glu_w, glu_b, conv_w, conv_b,
                  i, layer, nh_mem, cols, *, decode):
    m = z.shape[0]
    n = m // seq_len
    w, c = glu_w.shape[1], conv_w.shape[2]
    o_ua, o_ga, o_bg, o_cg, o_hb, o_gb, o_qm, o_gm = cols
    sps = seq_len if decode else seq_len // SUBLANES
    if decode:
        to_scan = lambda a: a.reshape(n, seq_len, -1).transpose(1, 0, 2)
        u = to_scan(z[:, o_ua:o_ua + w]).reshape(1, m, w)
        h0_re, h0_im = state_re.reshape(1, n, -1), state_im.reshape(1, n, -1)
    else:
        u = z[:, o_ua:o_ua + w].reshape(n, SUBLANES, sps, w).transpose(0, 2, 1, 3).reshape(n, seq_len, w)
        pad = lambda s: jnp.pad(s.reshape(n, 1, -1), ((0, 0), (0, SUBLANES - 1), (0, 0)))
        h0_re, h0_im = pad(state_re), pad(state_im)
    y, h_re, h_im = _s5(u, h0_re, h0_im, s5p, chain=not decode)
    if decode:
        y = y.reshape(seq_len, n, w).transpose(1, 0, 2).reshape(m, w)
        h_re, h_im = h_re[0], h_im[0]
    else:
        y = y.reshape(n, sps, SUBLANES, w).transpose(0, 2, 1, 3).reshape(m, w)
        h_re, h_im = h_re[:, SUBLANES - 1], h_im[:, SUBLANES - 1]
    mix_dtype = F32 if decode else BF16
    mix_a = _glu_gate(y, z, o_ga, glu_w, glu_b, i, tm=512, tn=512, out_dtype=mix_dtype)
    if decode:
        bg, cg, hb, gb = (to_scan(z[:, o:o + c]) for o in (o_bg, o_cg, o_hb, o_gb))
        mix_b, conv_new = _conv_gate_step(bg, cg, hb, gb, conv_state.transpose(1, 0, 2), conv_w, conv_b, i, ct=512)
        mix_b = mix_b.transpose(1, 0, 2).reshape(m, c)
        conv_new = conv_new.transpose(1, 0, 2)
    else:
        mix_b, conv_new = _conv_gate(z, (o_bg, o_cg, o_hb, o_gb), c, conv_state, conv_w, conv_b, i, seq_len,
                                     tm=512, ct=512)
    mix_m = _mem_attend(z, o_qm, o_gm, mem_k, mem_v, layer, seq_len, nh_mem, tq=512, joint=decode,
                        out_dtype=mix_dtype)
    gp = state_re.shape[1:]
    return (mix_a, mix_b, mix_m), (h_re.reshape((n,) + gp), h_im.reshape((n,) + gp), conv_new)


def _ssm_conv_layer(xp, xs, t_p, t_s, states_s, mem_p, mem_s, w_in, w_out, s5p, glu_w, glu_b, conv_w, conv_b,
                    ln_g, ln_b, i, layer, alpha, nh_mem):
    d = xp[0].shape[1]
    n_p = xp[0].shape[0] // t_p
    w, c, mw = glu_w.shape[1], conv_w.shape[2], mem_p[0].shape[2]
    cols, n_cols = _offsets((w, w, c, c, c, c, mw, mw))
    z_p, z_s = _matmul([(xp[1], xs[1], 0, w_in, i, 0, 0, d)], n_cols, tm=1024, tn=512, name="in_proj_ssm")
    zero_state = jnp.zeros((n_p,) + states_s[0].shape[1:], F32)
    zero_conv = jnp.zeros((n_p,) + states_s[2].shape[1:], F32)
    branch = (s5p, glu_w, glu_b, conv_w, conv_b, i, layer, nh_mem, cols)
    mix_p, st_p = _ssm_conv_mix(z_p, t_p, zero_state, zero_state, zero_conv, *mem_p, *branch, decode=False)
    mix_s, st_s = _ssm_conv_mix(z_s, t_s, *states_s, *mem_s, *branch, decode=True)
    y_p, y_s = _matmul([(mix_p[0], mix_s[0], 0, w_out, i, 0, 0, w), (mix_p[1], mix_s[1], 0, w_out, i, w // c, 0, c),
                        (mix_p[2], mix_s[2], 0, w_out, i, (w + c) // mw, 0, mw)],
                       d, tm=1024, tn=256, res=(xp[0], xs[0]), alpha=alpha, name="out_proj_ssm")
    return _layer_norm(y_p, ln_g, ln_b, layer, tm=256), _layer_norm(y_s, ln_g, ln_b, layer, tm=256), st_p, st_s


def _fox_layer(xp, xs, t_p, t_s, past, mem_p, mem_s, w_in_t, w_tail_t, b_f128, w_out, ln_g, ln_b, i, layer, alpha,
               nh, nh_mem):
    d = xp[0].shape[1]
    m_p, m_s = xp[0].shape[0], xs[0].shape[0]
    n_p, n_s = m_p // t_p, m_s // t_s
    mw = mem_p[0].shape[2]
    fw = w_out.shape[1] - mw
    hd = fw // nh
    proj = [_matmul([(xp[1], xs[1], 0, w_in_t, i, 0, o * fw, d)], fw, tm=1024, tn=512, transposed=True,
                    name="in_proj_fox") for o in range(4)]
    (zq_p, zq_s), (zk_p, zk_s), (zv_p, zv_s), (zg_p, zg_s) = proj
    tail_p, tail_s = _matmul([(xp[1], xs[1], 0, w_tail_t, i, 0, 0, d)], w_tail_t.shape[1], tm=1024, tn=384,
                             transposed=True, name="in_proj_fox_tail")
    o_qm, o_gm, o_f = 0, mw, 2 * mw

    tq = tk = min(512, t_p)
    lf_p, c, ck3, r = _forget_gates(tail_p, o_f, b_f128, i, n_p, t_p, tk=tk)
    mix_f_p = _fox_attend(zq_p, zk_p, zv_p, zg_p, c, ck3, r, n_p, t_p, nh, hd, tq=tq, tk=tk, hps=4)
    mix_m_p = _mem_attend(tail_p, o_qm, o_gm, *mem_p, layer, t_p, nh_mem, tq=512, joint=False, out_dtype=BF16)

    kc, vc, after, tot, page_table = past
    ng = nh // SUBLANES
    lf_s, c_new = _forget_gates_step(tail_s, o_f, b_f128, i, t_s)
    grouped = lambda a: (a.reshape(n_s, t_s, ng, SUBLANES, -1).transpose(0, 2, 1, 3, 4)
                         .reshape(n_s, ng, t_s * SUBLANES, -1))
    c_new = grouped(c_new[:, :nh]).reshape(n_s, ng, 1, t_s * SUBLANES)
    y = _fox_attend_step(grouped(zq_s), grouped(zk_s), grouped(zv_s), c_new, kc, vc, after, tot, page_table, i, t_s)
    y = y.reshape(n_s, ng, t_s, SUBLANES, hd).transpose(0, 2, 1, 3, 4).reshape(m_s, fw)
    mix_f_s = _gate_step(y, zg_s)
    mix_m_s = _mem_attend(tail_s, o_qm, o_gm, *mem_s, layer, t_s, nh_mem, tq=512, joint=True, out_dtype=F32)

    y_p, y_s = _matmul([(mix_f_p, mix_f_s, 0, w_out, i, 0, 0, fw), (mix_m_p, mix_m_s, 0, w_out, i, fw // mw, 0, mw)],
                       d, tm=1024, tn=256, res=(xp[0], xs[0]), alpha=alpha, name="out_proj_fox")
    kv_p = (zk_p.reshape(n_p, t_p, nh, hd), zv_p.reshape(n_p, t_p, nh, hd), lf_p[:, :, :nh])
    kv_s = (zk_s.reshape(n_s, t_s, nh, hd), zv_s.reshape(n_s, t_s, nh, hd), lf_s[:, :nh].reshape(n_s, t_s, nh))
    return _layer_norm(y_p, ln_g, ln_b, layer, tm=256), _layer_norm(y_s, ln_g, ln_b, layer, tm=256), kv_p, kv_s


def _gate_step_body(y_ref, g_ref, o_ref):
    o_ref[...] = y_ref[...] * _silu(g_ref[...])


def _gate_step(y, g):
    m, w = y.shape
    blk = pl.BlockSpec((m, w), lambda i: (0, 0))
    return pl.pallas_call(_gate_step_body, grid=(1,), in_specs=[blk, blk], out_specs=blk,
                          out_shape=jax.ShapeDtypeStruct((m, w), F32), compiler_params=_params("arbitrary"),
                          name="gate_step")(y, g)


def kernel(x_prompt, x_sample, cache_fox_k, cache_fox_v, cache_fox_lf, cache_mem_k, cache_mem_v, state_ssm_re, state_ssm_im, state_conv, page_table, mem_prompt, w_in_ssm, w_out_ssm, ssm_lam_re, ssm_lam_im, ssm_log_dt, ssm_b_re, ssm_b_im, ssm_c_re, ssm_c_im, ssm_d, glu_w, glu_b, conv_w, conv_b, w_in_fox, b_fgate, w_out_fox, w_mem_k, w_mem_v, ln_g, ln_b):
    n_p, t_p, d = x_prompt.shape
    n_s, t_s, _ = x_sample.shape
    depth = ln_g.shape[0]
    alpha = (2 * depth) ** 0.25
    nh_fox = b_fgate.shape[1]
    nh_mem = cache_mem_k.shape[3]
    mw = w_mem_k.shape[2]
    mem_len = mem_prompt.shape[1]
    fw = w_out_fox.shape[1] - mw
    n_fox, n_pool, ps = cache_fox_k.shape[:3]

    w_in_t = w_in_fox.transpose(0, 2, 1)
    w_tail_t = jnp.concatenate([w_in_t[:, 4 * fw + nh_fox:], w_in_t[:, 4 * fw:4 * fw + nh_fox]], axis=1)
    w_tail_t = jnp.pad(w_tail_t, ((0, 0), (0, LANES - nh_fox), (0, 0)))
    b_f128 = jnp.pad(b_fgate.astype(F32), ((0, 0), (0, LANES - nh_fox))).reshape(n_fox, 1, LANES)
    assert nh_fox % SUBLANES == 0
    ng = nh_fox // SUBLANES
    kc = cache_fox_k.reshape(n_fox, n_pool, ps, ng, SUBLANES, fw // nh_fox)
    vc = cache_fox_v.reshape(n_fox, n_pool, ps, ng, SUBLANES, fw // nh_fox)
    lfc = (cache_fox_lf.astype(F32).reshape(n_fox, n_pool, ps, ng, SUBLANES).transpose(0, 1, 3, 2, 4)
           .reshape(n_fox, n_pool * ng, ps * SUBLANES))
    lf_after, lf_tot = (a.reshape(n_fox, n_pool, ng, ps * SUBLANES) for a in _page_sums(lfc, rows=256))
    past = (kc, vc, lf_after, lf_tot, page_table)
    mem_s = (cache_mem_k.reshape(depth, n_s * mem_len, mw), cache_mem_v.reshape(depth, n_s * mem_len, mw))

    mem_in = mem_prompt.reshape(n_p * mem_len, d).astype(BF16)
    mem_p = (_memproj(mem_in, w_mem_k), _memproj(mem_in, w_mem_v))

    xp = (x_prompt.reshape(n_p * t_p, d),)
    xs = (x_sample.reshape(n_s * t_s, d),)
    xp, xs = xp + (xp[0].astype(BF16),), xs + (xs[0].astype(BF16),)

    fox_p, fox_s, ssm_p, ssm_s = [], [], [], []
    for layer in range(depth):
        i = layer // 2
        if layer % 2 == 0:
            s5p = _s5_discretise(ssm_lam_re[i], ssm_lam_im[i], ssm_log_dt[i], ssm_b_re[i], ssm_b_im[i],
                                 ssm_c_re[i], ssm_c_im[i], ssm_d[i])
            states_s = (state_ssm_re[i].astype(F32), state_ssm_im[i].astype(F32), state_conv[i].astype(F32))
            xp, xs, st_p, st_s = _ssm_conv_layer(xp, xs, t_p, t_s, states_s, mem_p, mem_s, w_in_ssm, w_out_ssm, s5p,
                                                 glu_w, glu_b, conv_w, conv_b, ln_g, ln_b, i, layer, alpha, nh_mem)
            ssm_p.append(st_p), ssm_s.append(st_s)
        else:
            xp, xs, kv_p, kv_s = _fox_layer(xp, xs, t_p, t_s, past, mem_p, mem_s, w_in_t, w_tail_t, b_f128,
                                            w_out_fox, ln_g, ln_b, i, layer, alpha, nh_fox, nh_mem)
            fox_p.append(kv_p), fox_s.append(kv_s)
    stack = lambda rows, k: jnp.stack([r[k] for r in rows])
    mem_shape = (depth, n_p, mem_len, nh_mem, mw // nh_mem)
    return (xp[0].reshape(n_p, t_p, d), xs[0].reshape(n_s, t_s, d),
            stack(fox_p, 0), stack(fox_p, 1), stack(fox_p, 2),
            stack(fox_s, 0), stack(fox_s, 1), stack(fox_s, 2),
            stack(ssm_p, 0), stack(ssm_p, 1), stack(ssm_p, 2),
            stack(ssm_s, 0), stack(ssm_s, 1), stack(ssm_s, 2),
            mem_p[0].reshape(mem_shape), mem_p[1].reshape(mem_shape))
```

```python
from functools import partial

import jax
import jax.numpy as jnp
from jax import lax
from jax.experimental import pallas as pl
from jax.experimental.pallas import tpu as pltpu

F32 = jnp.float32
BF16 = jnp.bfloat16

LN_EPS = 1e-5
LANES = 128
SUBLANES = 8
VMEM_LIMIT_BYTES = 56 * 1024 * 1024
NT_DIMS = (((1,), (1,)), ((), ()))
LOG2E = 1.4426950408889634


def _params(*sem):
    return pltpu.CompilerParams(dimension_semantics=sem, vmem_limit_bytes=VMEM_LIMIT_BYTES)


def _silu(g):
    return g * jax.nn.sigmoid(g)


def _mm_body(*refs, n_pairs, transposed, alpha, has_res, has_side):
    per = 3 if has_side else 2
    n_in = per * n_pairs + (per - 1 if has_res else 0)
    out_ref = refs[n_in]

    def product(x_of_pair, w_of_pair, res_ref):
        acc = None
        for p in range(n_pairs):
            x = x_of_pair(p)[...].astype(BF16)
            if transposed:
                d = lax.dot_general(x, w_of_pair(p), NT_DIMS, preferred_element_type=F32)
            else:
                d = jnp.dot(x, w_of_pair(p), preferred_element_type=F32)
            acc = d if acc is None else acc + d
        return acc if res_ref is None else alpha * res_ref[...] + acc

    res_ref = refs[per * n_pairs] if has_res else None
    if has_side:
        wb = refs[n_in + 2:]

        @pl.when(pl.program_id(1) == 0)
        def _():
            for p in range(n_pairs):
                wb[p][...] = refs[3 * p + 1][...].astype(BF16)
            refs[n_in + 1][...] = product(lambda p: refs[3 * p + 2], lambda p: wb[p][...],
                                          refs[3 * n_pairs + 1] if has_res else None)

        out_ref[...] = product(lambda p: refs[3 * p], lambda p: wb[p][...], res_ref).astype(out_ref.dtype)
    else:
        out_ref[...] = product(lambda p: refs[2 * p], lambda p: refs[2 * p + 1][...].astype(BF16),
                               res_ref).astype(out_ref.dtype)


def _matmul(pairs, n_out, *, tm, tn, res=None, alpha=1.0, out_dtype=F32, transposed=False, name="mm"):
    has_side = pairs[0][1] is not None
    m = pairs[0][0].shape[0]
    tm, tn = min(tm, m), min(tn, n_out)
    assert m % tm == 0 and n_out % tn == 0
    rc = (lambda f: (lambda j, i: f(i, j))) if has_side else (lambda f: f)
    in_specs, args, scratch = [], [], []
    for (x, xs, xkb, w, layer, wkb, col0, kdim) in pairs:
        assert col0 % tn == 0
        cb = col0 // tn
        in_specs.append(pl.BlockSpec((tm, kdim), rc(lambda i, j, xkb=xkb: (i, xkb))))
        if transposed:
            in_specs.append(pl.BlockSpec((None, tn, kdim), rc(lambda i, j, l=layer, wkb=wkb, cb=cb: (l, cb + j, wkb))))
            scratch.append(pltpu.VMEM((tn, kdim), BF16))
        else:
            in_specs.append(pl.BlockSpec((None, kdim, tn), rc(lambda i, j, l=layer, wkb=wkb, cb=cb: (l, wkb, cb + j))))
            scratch.append(pltpu.VMEM((kdim, tn), BF16))
        args += [x, w]
        if has_side:
            in_specs.append(pl.BlockSpec((xs.shape[0], kdim), rc(lambda i, j, xkb=xkb: (0, xkb))))
            args.append(xs)
    out_specs = [pl.BlockSpec((tm, tn), rc(lambda i, j: (i, j)))]
    out_shape = [jax.ShapeDtypeStruct((m, n_out), out_dtype)]
    if res is not None:
        in_specs.append(pl.BlockSpec((tm, tn), rc(lambda i, j: (i, j))))
        args.append(res[0] if has_side else res)
    if has_side:
        ms = pairs[0][1].shape[0]
        if res is not None:
            in_specs.append(pl.BlockSpec((ms, tn), rc(lambda i, j: (0, j))))
            args.append(res[1])
        out_specs.append(pl.BlockSpec((ms, tn), rc(lambda i, j: (0, j))))
        out_shape.append(jax.ShapeDtypeStruct((ms, n_out), F32))
    out = pl.pallas_call(
        partial(_mm_body, n_pairs=len(pairs), transposed=transposed, alpha=alpha, has_res=res is not None,
                has_side=has_side),
        grid=(n_out // tn, m // tm) if has_side else (m // tm, n_out // tn),
        in_specs=in_specs,
        out_specs=out_specs,
        out_shape=out_shape,
        scratch_shapes=scratch if has_side else [],
        compiler_params=_params("parallel", "arbitrary"),
        name=name,
    )(*args)
    return out if has_side else out[0]


def _memproj_body(x_ref, w_ref, o_ref):
    o_ref[...] = jnp.dot(x_ref[...].astype(BF16), w_ref[...].astype(BF16), preferred_element_type=F32)


def _memproj(x, w):
    nl, d, n = w.shape
    r = x.shape[0]
    return pl.pallas_call(
        _memproj_body,
        grid=(nl,),
        in_specs=[pl.BlockSpec((r, d), lambda l: (0, 0)), pl.BlockSpec((None, d, n), lambda l: (l, 0, 0))],
        out_specs=pl.BlockSpec((None, r, n), lambda l: (l, 0, 0)),
        out_shape=jax.ShapeDtypeStruct((nl, r, n), F32),
        compiler_params=_params("arbitrary"),
        name="memproj",
    )(x, w)


def _ln_body(y_ref, g_ref, b_ref, xf_ref, xb_ref):
    y = y_ref[...]
    mu = jnp.mean(y, axis=-1, keepdims=True)
    d = y - mu
    var = jnp.mean(d * d, axis=-1, keepdims=True)
    out = d * lax.rsqrt(var + LN_EPS) * g_ref[...] + b_ref[...]
    xf_ref[...] = out
    xb_ref[...] = out.astype(BF16)


def _layer_norm(y, g, b, layer, *, tm):
    m, d = y.shape
    tm = min(tm, m)
    g3 = g.reshape(g.shape[0], 1, d)
    b3 = b.reshape(b.shape[0], 1, d)
    vec = pl.BlockSpec((None, 1, d), lambda i: (layer, 0, 0))
    row = pl.BlockSpec((tm, d), lambda i: (i, 0))
    return pl.pallas_call(
        _ln_body,
        grid=(m // tm,),
        in_specs=[row, vec, vec],
        out_specs=[row, row],
        out_shape=[jax.ShapeDtypeStruct((m, d), F32), jax.ShapeDtypeStruct((m, d), BF16)],
        compiler_params=_params("parallel"),
        name="layernorm",
    )(y, g3, b3)


def _s5_discretise(lam_re, lam_im, log_dt, b_re, b_im, c_re, c_im, d_skip):
    g, p, h = b_re.shape
    gpc = LANES // h
    ncc = g // gpc
    lam_re, lam_im = lam_re.astype(F32), lam_im.astype(F32)
    dt = jnp.exp(log_dt.astype(F32))[:, None]
    mag = jnp.exp(lam_re * dt)
    ar = mag * jnp.cos(lam_im * dt)
    ai = mag * jnp.sin(lam_im * dt)
    nr = ar - 1.0
    den = jnp.square(lam_re) + jnp.square(lam_im)
    cr = ((nr * lam_re + ai * lam_im) / den)[..., None]
    ci = ((ai * lam_re - nr * lam_im) / den)[..., None]
    b_re, b_im = b_re.astype(F32), b_im.astype(F32)
    bb_re = cr * b_re - ci * b_im
    bb_im = cr * b_im + ci * b_re
    eye = jnp.eye(gpc, dtype=F32)

    def pack_in(bb):
        return jnp.einsum('cgph,gk->cghkp', bb.reshape(ncc, gpc, p, h), eye).reshape(ncc, gpc * h, gpc * p)

    def pack_out(c):
        return jnp.einsum('cghp,gk->cgpkh', c.astype(F32).reshape(ncc, gpc, h, p), eye).reshape(ncc, gpc * p, gpc * h)

    return dict(a_re=ar.reshape(ncc, 1, gpc * p), a_im=ai.reshape(ncc, 1, gpc * p),
                wb_re=pack_in(bb_re), wb_im=pack_in(bb_im), wc_re=pack_out(c_re), wc_im=pack_out(c_im),
                d=d_skip.astype(F32).reshape(ncc, 1, gpc * h))


def _s5_body(u_ref, h0r_ref, h0i_ref, ar_ref, ai_ref, wbr_ref, wbi_ref, wcr_ref, wci_ref, d_ref,
             y_ref, hr_out, hi_out, hre, him, *, steps, chain):
    u = u_ref[...]
    ub = u.astype(BF16)
    hre[...] = jnp.dot(ub, wbr_ref[...].astype(BF16), preferred_element_type=F32)
    him[...] = jnp.dot(ub, wbi_ref[...].astype(BF16), preferred_element_type=F32)
    ns = hre.shape[1]
    ar = jnp.broadcast_to(ar_ref[...], (SUBLANES, ns))
    ai = jnp.broadcast_to(ai_ref[...], (SUBLANES, ns))

    def rows_of(t):
        return pl.ds(pl.multiple_of(t * SUBLANES, SUBLANES), SUBLANES)

    def scan_step(t, c):
        hr, hi = c
        rows = rows_of(t)
        nr = ar * hr - ai * hi + hre[rows, :]
        ni = ar * hi + ai * hr + him[rows, :]
        hre[rows, :] = nr
        him[rows, :] = ni
        return nr, ni

    fr, fi = lax.fori_loop(0, steps, scan_step, (h0r_ref[...], h0i_ref[...]), unroll=2)

    if chain:
        alr, ali = None, None
        sqr, sqi, e = ar, ai, steps
        while e:
            if e & 1:
                alr, ali = (sqr, sqi) if alr is None else (alr * sqr - ali * sqi, alr * sqi + ali * sqr)
            sqr, sqi = sqr * sqr - sqi * sqi, 2.0 * sqr * sqi
            e >>= 1
        sub = lax.broadcasted_iota(jnp.int32, (SUBLANES, ns), 0)
        sr, si = fr, fi
        for c in range(1, SUBLANES):
            pr, pi = pltpu.roll(sr, 1, 0), pltpu.roll(si, 1, 0)
            sr = jnp.where(sub == c, alr * pr - ali * pi + fr, sr)
            si = jnp.where(sub == c, alr * pi + ali * pr + fi, si)
        cin_r = jnp.where(sub == 0, 0.0, pltpu.roll(sr, 1, 0))
        cin_i = jnp.where(sub == 0, 0.0, pltpu.roll(si, 1, 0))

        def fix_step(t, c):
            gr, gi = c
            rows = rows_of(t)
            nr, ni = ar * gr - ai * gi, ar * gi + ai * gr
            hre[rows, :] = hre[rows, :] + nr
            him[rows, :] = him[rows, :] + ni
            return nr, ni

        lax.fori_loop(0, steps, fix_step, (cin_r, cin_i), unroll=2)
        fr, fi = sr, si

    hr_out[...] = fr
    hi_out[...] = fi
    y = (jnp.dot(hre[...].astype(BF16), wcr_ref[...].astype(BF16), preferred_element_type=F32)
         - jnp.dot(him[...].astype(BF16), wci_ref[...].astype(BF16), preferred_element_type=F32)
         + d_ref[...] * u)
    y_ref[...] = y


def _s5(u, h0_re, h0_im, prm, *, chain):
    nb, rows, w = u.shape
    steps = rows // SUBLANES
    ncc, cw, ns = prm["wb_re"].shape
    assert w == ncc * cw and cw == LANES
    seq = pl.BlockSpec((None, rows, cw), lambda b, c: (b, 0, c))
    st = pl.BlockSpec((None, SUBLANES, ns), lambda b, c: (b, 0, c))

    def par(shape):
        return pl.BlockSpec((None,) + shape, lambda b, c: (c, 0, 0))

    return pl.pallas_call(
        partial(_s5_body, steps=steps, chain=chain),
        grid=(nb, ncc),
        in_specs=[seq, st, st, par((1, ns)), par((1, ns)), par((cw, ns)), par((cw, ns)),
                  par((ns, cw)), par((ns, cw)), par((1, cw))],
        out_specs=[seq, st, st],
        out_shape=[jax.ShapeDtypeStruct((nb, rows, w), F32),
                   jax.ShapeDtypeStruct((nb, SUBLANES, ncc * ns), F32),
                   jax.ShapeDtypeStruct((nb, SUBLANES, ncc * ns), F32)],
        scratch_shapes=[pltpu.VMEM((rows, ns), F32), pltpu.VMEM((rows, ns), F32)],
        compiler_params=_params("parallel", "arbitrary"),
        name="s5_scan",
    )(u, h0_re, h0_im, prm["a_re"], prm["a_im"], prm["wb_re"], prm["wb_im"], prm["wc_re"], prm["wc_im"], prm["d"])


def _glu_body(yfull_ref, yblk_ref, ga_ref, w_ref, b_ref, o_ref, zb):
    @pl.when(pl.program_id(1) == 0)
    def _():
        zb[...] = jax.nn.gelu(yfull_ref[...]).astype(BF16)

    glu = jnp.dot(zb[...], w_ref[...].astype(BF16), preferred_element_type=F32) + b_ref[...]
    z = jax.nn.gelu(yblk_ref[...])
    o_ref[...] = (z * jax.nn.sigmoid(glu) * _silu(ga_ref[...])).astype(o_ref.dtype)


def _glu_gate(y, z_all, ga_col0, glu_w, glu_b, layer, *, tm, tn, out_dtype):
    m, w = y.shape
    tm, tn = min(tm, m), min(tn, w)
    assert ga_col0 % tn == 0
    gb3 = glu_b.reshape(glu_b.shape[0], 1, w)
    return pl.pallas_call(
        _glu_body,
        grid=(m // tm, w // tn),
        in_specs=[pl.BlockSpec((tm, w), lambda i, j: (i, 0)),
                  pl.BlockSpec((tm, tn), lambda i, j: (i, j)),
                  pl.BlockSpec((tm, tn), lambda i, j: (i, ga_col0 // tn + j)),
                  pl.BlockSpec((None, w, tn), lambda i, j: (layer, 0, j)),
                  pl.BlockSpec((None, 1, tn), lambda i, j: (layer, 0, j))],
        out_specs=pl.BlockSpec((tm, tn), lambda i, j: (i, j)),
        out_shape=jax.ShapeDtypeStruct((m, w), out_dtype),
        scratch_shapes=[pltpu.VMEM((tm, w), BF16)],
        compiler_params=_params("parallel", "arbitrary"),
        name="glu_gate",
    )(y, y, z_all, glu_w, gb3)


def _conv_body(bg_ref, cg_ref, hb_ref, gb_ref, cs_ref, w_ref, b_ref, o_ref, cn_ref, prev, *, tiles_per_seq):
    @pl.when(pl.program_id(1) % tiles_per_seq == 0)
    def _():
        prev[...] = cs_ref[...]

    v = cg_ref[...] * hb_ref[...]
    tm = v.shape[0]
    p2, p1 = prev[0:1, :], prev[1:2, :]
    row = lax.broadcasted_iota(jnp.int32, v.shape, 0)
    v1 = jnp.where(row == 0, p1, pltpu.roll(v, 1, 0))
    v2 = jnp.where(row == 0, p2, jnp.where(row == 1, p1, pltpu.roll(v, 2, 0)))
    y = b_ref[...] + w_ref[0:1, :] * v2 + w_ref[1:2, :] * v1 + w_ref[2:3, :] * v
    o_ref[...] = (bg_ref[...] * y * _silu(gb_ref[...])).astype(o_ref.dtype)
    last = v[tm - 2:tm, :]
    prev[...] = last
    cn_ref[...] = last


def _conv_gate(z_all, col0, c, conv_state, conv_w, conv_b, layer, seq_len, *, tm, ct):
    m = z_all.shape[0]
    tm, ct = min(tm, seq_len), min(ct, c)
    assert seq_len % tm == 0 and all(o % ct == 0 for o in col0) and conv_w.shape[1] == 3
    tps = seq_len // tm
    n = m // seq_len
    cb3 = conv_b.reshape(conv_b.shape[0], 1, c)

    def zcol(o):
        return pl.BlockSpec((tm, ct), lambda j, i, o=o: (i, o // ct + j))

    st = pl.BlockSpec((None, 2, ct), lambda j, i: (i // tps, 0, j))
    return pl.pallas_call(
        partial(_conv_body, tiles_per_seq=tps),
        grid=(c // ct, m // tm),
        in_specs=[zcol(col0[0]), zcol(col0[1]), zcol(col0[2]), zcol(col0[3]), st,
                  pl.BlockSpec((None, 3, ct), lambda j, i: (layer, 0, j)),
                  pl.BlockSpec((None, 1, ct), lambda j, i: (layer, 0, j))],
        out_specs=[pl.BlockSpec((tm, ct), lambda j, i: (i, j)), st],
        out_shape=[jax.ShapeDtypeStruct((m, c), BF16), jax.ShapeDtypeStruct((n, 2, c), F32)],
        scratch_shapes=[pltpu.VMEM((2, ct), F32)],
        compiler_params=_params("parallel", "arbitrary"),
        name="conv_gate",
    )(z_all, z_all, z_all, z_all, conv_state, conv_w, cb3)


def _conv_step_body(bg_ref, cg_ref, hb_ref, gb_ref, cs_ref, w_ref, b_ref, o_ref, cn_ref, *, steps):
    vp = [cs_ref[0], cs_ref[1]] + [cg_ref[t] * hb_ref[t] for t in range(steps)]
    for t in range(steps):
        y = b_ref[...] + w_ref[0:1, :] * vp[t] + w_ref[1:2, :] * vp[t + 1] + w_ref[2:3, :] * vp[t + 2]
        o_ref[t] = bg_ref[t] * y * _silu(gb_ref[t])
    cn_ref[0] = vp[steps]
    cn_ref[1] = vp[steps + 1]


def _conv_gate_step(bg, cg, hb, gb, conv_state_t, conv_w, conv_b, layer, *, ct):
    steps, n, c = bg.shape
    ct = min(ct, c)
    assert conv_w.shape[1] == 3
    cb3 = conv_b.reshape(conv_b.shape[0], 1, c)
    blk = pl.BlockSpec((steps, n, ct), lambda j: (0, 0, j))
    st = pl.BlockSpec((2, n, ct), lambda j: (0, 0, j))
    return pl.pallas_call(
        partial(_conv_step_body, steps=steps),
        grid=(c // ct,),
        in_specs=[blk, blk, blk, blk, st,
                  pl.BlockSpec((None, 3, ct), lambda j: (layer, 0, j)),
                  pl.BlockSpec((None, 1, ct), lambda j: (layer, 0, j))],
        out_specs=[blk, st],
        out_shape=[jax.ShapeDtypeStruct((steps, n, c), F32), jax.ShapeDtypeStruct((2, n, c), F32)],
        compiler_params=_params("parallel"),
        name="conv_gate_step",
    )(bg, cg, hb, gb, conv_state_t, conv_w, cb3)


def _mem_body(q_ref, g_ref, k_ref, v_ref, o_ref, *, nh, hd, rows_per_seq, keys_per_seq):
    scale = hd ** -0.5
    tq, nk = q_ref.shape[0], k_ref.shape[0]
    if rows_per_seq is not None:
        rseq = lax.broadcasted_iota(jnp.int32, (tq, nk), 0) // rows_per_seq
        kseq = lax.broadcasted_iota(jnp.int32, (tq, nk), 1) // keys_per_seq
        own = rseq == kseq
    for h in range(nh):
        sl = slice(h * hd, (h + 1) * hd)
        q = q_ref[:, sl].astype(BF16)
        k = k_ref[:, sl].astype(BF16)
        v = v_ref[:, sl].astype(BF16)
        s = lax.dot_general(q, k, NT_DIMS, preferred_element_type=F32) * scale
        if rows_per_seq is not None:
            s = jnp.where(own, s, -jnp.inf)
        e = jnp.exp(s - jnp.max(s, axis=-1, keepdims=True))
        p = e / jnp.sum(e, axis=-1, keepdims=True)
        o = jnp.dot(p.astype(BF16), v, preferred_element_type=F32)
        o_ref[:, sl] = (o * _silu(g_ref[:, sl])).astype(o_ref.dtype)


def _mem_attend(z_all, q_col0, g_col0, mem_k, mem_v, layer, seq_len, nh, *, tq, joint, out_dtype):
    m = z_all.shape[0]
    mw = mem_k.shape[2]
    n = m // seq_len
    mem_len = mem_k.shape[1] // n
    assert q_col0 % mw == 0 and g_col0 % mw == 0
    if joint:
        tq, kv_rows, rps = m, n * mem_len, seq_len
        kv_idx = lambda i: (layer, 0, 0)
    else:
        tq = min(tq, seq_len)
        assert seq_len % tq == 0
        tps = seq_len // tq
        kv_rows, rps = mem_len, None
        kv_idx = lambda i: (layer, i // tps, 0)
    kv = pl.BlockSpec((None, kv_rows, mw), kv_idx)
    return pl.pallas_call(
        partial(_mem_body, nh=nh, hd=mw // nh, rows_per_seq=rps, keys_per_seq=mem_len),
        grid=(m // tq,),
        in_specs=[pl.BlockSpec((tq, mw), lambda i: (i, q_col0 // mw)),
                  pl.BlockSpec((tq, mw), lambda i: (i, g_col0 // mw)), kv, kv],
        out_specs=pl.BlockSpec((tq, mw), lambda i: (i, 0)),
        out_shape=jax.ShapeDtypeStruct((m, mw), out_dtype),
        compiler_params=_params("parallel"),
        name="mem_attend",
    )(z_all, z_all, mem_k, mem_v)


def _log_sigmoid(x):
    return jnp.minimum(x, 0.0) - jnp.log1p(jnp.exp(-jnp.abs(x)))


def _split3(x):
    hi = x.astype(BF16)
    r = x - hi.astype(F32)
    mid = r.astype(BF16)
    return hi, mid, (r - mid.astype(F32)).astype(BF16)


def _lf_body(f_ref, bf_ref, lf_ref, c_ref, ck3_ref, r_ref, *, tk):
    lf = _log_sigmoid(f_ref[...] + bf_ref[...])
    lf_ref[...] = lf
    t = lf.shape[0]
    row = lax.broadcasted_iota(jnp.int32, lf.shape, 0)
    c, s = lf, 1
    while s < t:
        c = c + jnp.where(row >= s, pltpu.roll(c, s, 0), 0.0)
        s *= 2
    c_ref[...] = c
    for j in range(t // tk):
        rows = slice(j * tk, (j + 1) * tk)
        r = c[(j + 1) * tk - 1:(j + 1) * tk, :]
        r_ref[j:j + 1, :] = r
        for part, term in enumerate(_split3((r - c[rows, :]) * LOG2E)):
            ck3_ref[rows, part * LANES:(part + 1) * LANES] = term


def _forget_gates(z_tail, f_col0, b_f128, layer, n, seq_len, *, tk):
    assert f_col0 % LANES == 0 and seq_len % tk == 0
    z3 = z_tail.reshape(n, seq_len, z_tail.shape[1])
    blk = pl.BlockSpec((None, seq_len, LANES), lambda b: (b, 0, 0))
    return pl.pallas_call(
        partial(_lf_body, tk=tk),
        grid=(n,),
        in_specs=[pl.BlockSpec((None, seq_len, LANES), lambda b: (b, 0, f_col0 // LANES)),
                  pl.BlockSpec((None, 1, LANES), lambda b: (layer, 0, 0))],
        out_specs=[blk, blk, pl.BlockSpec((None, seq_len, 3 * LANES), lambda b: (b, 0, 0)),
                   pl.BlockSpec((None, seq_len // tk, LANES), lambda b: (b, 0, 0))],
        out_shape=[jax.ShapeDtypeStruct((n, seq_len, LANES), F32), jax.ShapeDtypeStruct((n, seq_len, LANES), F32),
                   jax.ShapeDtypeStruct((n, seq_len, 3 * LANES), BF16),
                   jax.ShapeDtypeStruct((n, seq_len // tk, LANES), F32)],
        compiler_params=_params("parallel"),
        name="forget_gates",
    )(z3, b_f128)


def _lf_step_body(f_ref, bf_ref, lf_ref, c_ref, *, steps):
    lf = _log_sigmoid(f_ref[...] + bf_ref[...])
    lf_ref[...] = lf
    step = lax.broadcasted_iota(jnp.int32, lf.shape, 0) % steps
    c, s = lf, 1
    while s < steps:
        c = c + jnp.where(step >= s, pltpu.roll(c, s, 0), 0.0)
        s *= 2
    c_ref[...] = c


def _forget_gates_step(z_tail, f_col0, b_f128, layer, steps):
    m = z_tail.shape[0]
    out = pl.BlockSpec((m, LANES), lambda i: (0, 0))
    return pl.pallas_call(
        partial(_lf_step_body, steps=steps),
        grid=(1,),
        in_specs=[pl.BlockSpec((m, LANES), lambda i: (0, f_col0 // LANES)),
                  pl.BlockSpec((None, 1, LANES), lambda i: (layer, 0, 0))],
        out_specs=[out, out],
        out_shape=[jax.ShapeDtypeStruct((m, LANES), F32), jax.ShapeDtypeStruct((m, LANES), F32)],
        compiler_params=_params("arbitrary"),
        name="forget_gates_step",
    )(z_tail, b_f128)


def _fox_body(q_ref, k_ref, v_ref, g_ref, c_ref, ck3_ref, r_ref, o_ref, ka, va, m_s, acc, *, tq, tk, hd, hps):
    h0 = pl.program_id(1) * hps
    qi = pl.program_id(2)
    seq_len = k_ref.shape[0]
    cols = lambda u: slice(u * hd, (u + 1) * hd)

    @pl.when(qi == 0)
    def _():
        sel_r = lax.broadcasted_iota(jnp.int32, (3 * LANES, hd), 0)
        sel_c = lax.broadcasted_iota(jnp.int32, (3 * LANES, hd), 1)
        one0 = jnp.where(lax.broadcasted_iota(jnp.int32, (tq, hd), 1) == 0, 1.0, 0.0).astype(BF16)
        for u in range(hps):
            sel = ((sel_r % LANES == h0 + u) & (sel_r // LANES == sel_c)).astype(BF16)
            for r0 in range(0, seq_len, tq):
                rows = slice(r0, r0 + tq)
                ka[u, rows, :hd] = k_ref[rows, cols(u)].astype(BF16)
                ka[u, rows, hd:] = jnp.dot(ck3_ref[rows, :], sel, preferred_element_type=F32).astype(BF16)
                va[u, rows, :hd] = v_ref[rows, cols(u)].astype(BF16)
                va[u, rows, hd:] = one0

    lane = lax.broadcasted_iota(jnp.int32, c_ref.shape, 1)
    lane1 = lax.broadcasted_iota(jnp.int32, (1, LANES), 1)
    ones3 = jnp.where(lax.broadcasted_iota(jnp.int32, (tq, hd), 1) < 3, 1.0, 0.0).astype(BF16)
    cq, qa = [], []
    for u in range(hps):
        cq.append(jnp.broadcast_to(jnp.sum(jnp.where(lane == h0 + u, c_ref[...], 0.0), axis=1, keepdims=True),
                                   (tq, LANES)))
        qa.append(jnp.concatenate([(q_ref[:, cols(u)] * (hd ** -0.5 * LOG2E)).astype(BF16), ones3], axis=1))
    m_s[...] = jnp.full(m_s.shape, -jnp.inf, F32)
    acc[...] = jnp.zeros(acc.shape, F32)
    reps = tk // LANES

    def kv_step(j, masked):
        ks = pl.ds(pl.multiple_of(j * tk, tk), tk)
        rj_all = r_ref[pl.ds(j, 1), :]
        for u in range(hps):
            s = lax.dot_general(qa[u], ka[u, ks, :], NT_DIMS, preferred_element_type=F32)
            if masked:
                key_minus_query = (lax.broadcasted_iota(jnp.int32, (tq, tk), 1)
                                   - lax.broadcasted_iota(jnp.int32, (tq, tk), 0))
                s = jnp.where(key_minus_query <= qi * tq - j * tk, s, -jnp.inf)
            rj = jnp.sum(jnp.where(lane1 == h0 + u, rj_all, 0.0), axis=1, keepdims=True)
            shift = (cq[u] - rj) * LOG2E
            m_old = m_s[u]
            m_new = jnp.maximum(m_old, jnp.broadcast_to(jnp.max(s, axis=1, keepdims=True), (tq, LANES)) + shift)
            p = jnp.exp2(s - jnp.concatenate([m_new - shift] * reps, axis=1))
            a = jnp.exp2(m_old - m_new)
            acc[u] = (jnp.concatenate([a, a], axis=1) * acc[u]
                      + jnp.dot(p.astype(BF16), va[u, ks, :], preferred_element_type=F32))
            m_s[u] = m_new

    n_full = (qi * tq) // tk
    lax.fori_loop(0, n_full, lambda j, c: (kv_step(j, False), c)[1], 0)
    for d in range(tq // tk):
        kv_step(n_full + d, True)
    for u in range(hps):
        o = acc[u]
        o_ref[:, cols(u)] = (o[:, :hd] / jnp.broadcast_to(o[:, hd:hd + 1], (tq, hd))
                             * _silu(g_ref[:, cols(u)])).astype(o_ref.dtype)


def _fox_attend(zq, zk, zv, zg, c, ck3, r, n, seq_len, nh, hd, *, tq, tk, hps):
    assert seq_len % tq == 0 and tq % tk == 0 and r.shape[1] == seq_len // tk and hd == LANES and nh % hps == 0
    r3 = lambda a: a.reshape(n, seq_len, nh * hd)
    qblk = pl.BlockSpec((None, tq, hps * hd), lambda b, h, i: (b, i, h))
    kvblk = pl.BlockSpec((None, seq_len, hps * hd), lambda b, h, i: (b, 0, h))
    return pl.pallas_call(
        partial(_fox_body, tq=tq, tk=tk, hd=hd, hps=hps),
        grid=(n, nh // hps, seq_len // tq),
        in_specs=[qblk, kvblk, kvblk, qblk,
                  pl.BlockSpec((None, tq, LANES), lambda b, h, i: (b, i, 0)),
                  pl.BlockSpec((None, seq_len, 3 * LANES), lambda b, h, i: (b, 0, 0)),
                  pl.BlockSpec((None, seq_len // tk, LANES), lambda b, h, i: (b, 0, 0))],
        out_specs=qblk,
        out_shape=jax.ShapeDtypeStruct((n, seq_len, nh * hd), BF16),
        scratch_shapes=[pltpu.VMEM((hps, seq_len, 2 * hd), BF16), pltpu.VMEM((hps, seq_len, 2 * hd), BF16),
                        pltpu.VMEM((hps, tq, LANES), F32), pltpu.VMEM((hps, tq, 2 * hd), F32)],
        compiler_params=_params("parallel", "parallel", "arbitrary"),
        name="fox_attend",
    )(r3(zq), r3(zk), r3(zv), r3(zg), c, ck3, r).reshape(n * seq_len, nh * hd)


def _page_sums_body(lf_ref, after_ref, tot_ref, *, stride):
    lf = lf_ref[...]
    keys = lf.shape[1]
    lane = lax.broadcasted_iota(jnp.int32, lf.shape, 1)
    sfx, s = lf, stride
    while s < keys:
        sfx = sfx + jnp.where(lane + s < keys, pltpu.roll(sfx, keys - s, 1), 0.0)
        s *= 2
    after_ref[...] = sfx - lf
    tot, s = jnp.where(lane < stride, sfx, 0.0), stride
    while s < keys:
        tot = tot + pltpu.roll(tot, s, 1)
        s *= 2
    tot_ref[...] = tot


def _page_sums(lfc, *, rows):
    nl, n_rows, keys = lfc.shape
    rows = min(rows, n_rows)
    assert n_rows % rows == 0
    blk = pl.BlockSpec((None, rows, keys), lambda l, i: (l, i, 0))
    out = jax.ShapeDtypeStruct(lfc.shape, F32)
    return pl.pallas_call(
        partial(_page_sums_body, stride=SUBLANES),
        grid=(nl, n_rows // rows),
        in_specs=[blk], out_specs=[blk, blk], out_shape=[out, out],
        compiler_params=_params("parallel", "parallel"),
        name="page_forget_sums",
    )(lfc)


def _fox_step_body(pt_ref, q_ref, kn_ref, vn_ref, cn_ref, *rest, steps, hd, n_steps, pps):
    kc_refs, vc_refs, after_refs, tot_refs = (rest[k * pps:(k + 1) * pps] for k in range(4))
    o_ref, qb, own_s, m_s, l_s, acc, run = rest[4 * pps:]
    p = pl.program_id(1)
    n_groups = q_ref.shape[0]
    rows = steps * SUBLANES
    keys = kc_refs[0].shape[0] * SUBLANES

    def same_head(n_keys):
        r = lax.broadcasted_iota(jnp.int32, (rows, n_keys), 0) % SUBLANES
        c = lax.broadcasted_iota(jnp.int32, (rows, n_keys), 1) % SUBLANES
        return jnp.where(r == c, 0.0, -jnp.inf)

    def attend(k_b, v_b, bias):
        groups = range(n_groups)
        s = [lax.dot_general(qb[g], k_b[g], NT_DIMS, preferred_element_type=F32) + bias[g] for g in groups]
        m_old = [m_s[g] for g in groups]
        m_new = [jnp.maximum(m_old[g], jnp.max(s[g], axis=1, keepdims=True)) for g in groups]
        e = [jnp.exp2(s[g] - m_new[g]) for g in groups]
        a = [jnp.exp2(m_old[g] - m_new[g]) for g in groups]
        pv = [jnp.dot(e[g].astype(BF16), v_b[g], preferred_element_type=F32) for g in groups]
        for g in groups:
            l_s[g] = a[g] * l_s[g] + jnp.sum(e[g], axis=1, keepdims=True)
            acc[g] = a[g] * acc[g] + pv[g]
            m_s[g] = m_new[g]

    @pl.when(p == 0)
    def _():
        m_s[...] = jnp.full(m_s.shape, -jnp.inf, F32)
        l_s[...] = jnp.zeros(l_s.shape, F32)
        acc[...] = jnp.zeros(acc.shape, F32)
        run[...] = jnp.zeros(run.shape, F32)
        own_s[...] = same_head(keys)
        key_step = lax.broadcasted_iota(jnp.int32, (rows, rows), 1) // SUBLANES
        q_step = lax.broadcasted_iota(jnp.int32, (rows, rows), 0) // SUBLANES
        visible = jnp.where(key_step <= q_step, same_head(rows), -jnp.inf)
        for g in range(n_groups):
            qb[g] = (q_ref[g] * (hd ** -0.5 * LOG2E)).astype(BF16)
        attend([kn_ref[g].astype(BF16) for g in range(n_groups)],
               [vn_ref[g].astype(BF16) for g in range(n_groups)],
               [visible - LOG2E * cn_ref[g] for g in range(n_groups)])

    running, bias = run[...], []
    for k in range(pps):
        bias.append(LOG2E * (after_refs[k][...] + running))
        running = running + tot_refs[k][...]
    run[...] = running
    page_rows = lambda refs, g: jnp.concatenate([r[:, g].reshape(keys, hd).astype(BF16) for r in refs], axis=0)
    attend([page_rows(kc_refs, g) for g in range(n_groups)],
           [page_rows(vc_refs, g) for g in range(n_groups)],
           [jnp.concatenate([own_s[...] + b[g:g + 1, :] for b in bias], axis=1) for g in range(n_groups)])

    @pl.when(p == n_steps - 1)
    def _():
        o_ref[...] = acc[...] / l_s[...]


def _fox_attend_step(q, k_new, v_new, c_new, kc, vc, after, tot, page_table, layer, steps, *, pps):
    n, n_groups, rows, hd = q.shape
    ps = kc.shape[2]
    keys = ps * SUBLANES
    n_pages = page_table.shape[1]
    assert rows == steps * SUBLANES and kc.shape[3:] == (n_groups, SUBLANES, hd) and keys % LANES == 0
    assert n_pages % pps == 0
    per_seq = lambda shape: pl.BlockSpec((None,) + shape, lambda b, p, pt: (b,) + (0,) * len(shape))

    def pages(shape):
        return [pl.BlockSpec((None, None) + shape,
                             lambda b, p, pt, k=k: (layer, pt[(b + 1) * n_pages - 1 - p * pps - k]) + (0,) * len(shape))
                for k in range(pps)]

    grid_spec = pltpu.PrefetchScalarGridSpec(
        num_scalar_prefetch=1,
        grid=(n, n_pages // pps),
        in_specs=[per_seq((n_groups, rows, hd)), per_seq((n_groups, rows, hd)), per_seq((n_groups, rows, hd)),
                  per_seq((n_groups, 1, rows)),
                  *pages((ps, n_groups, SUBLANES, hd)), *pages((ps, n_groups, SUBLANES, hd)),
                  *pages((n_groups, keys)), *pages((n_groups, keys))],
        out_specs=per_seq((n_groups, rows, hd)),
        scratch_shapes=[pltpu.VMEM((n_groups, rows, hd), BF16), pltpu.VMEM((rows, keys), F32),
                        pltpu.VMEM((n_groups, rows, 1), F32), pltpu.VMEM((n_groups, rows, 1), F32),
                        pltpu.VMEM((n_groups, rows, hd), F32), pltpu.VMEM((n_groups, keys), F32)],
    )
    return pl.pallas_call(
        partial(_fox_step_body, steps=steps, hd=hd, n_steps=n_pages // pps, pps=pps),
        grid_spec=grid_spec,
        out_shape=jax.ShapeDtypeStruct((n, n_groups, rows, hd), F32),
        compiler_params=_params("arbitrary", "arbitrary"),
        name="fox_attend_step",
    )(page_table.reshape(-1), q, k_new, v_new, c_new, *([kc] * pps), *([vc] * pps), *([after] * pps), *([tot] * pps))


def _offsets(sizes):
    out, o = [], 0
    for s in sizes:
        out.append(o)
        o += s
    return out, o


def _ssm_conv_mix(z, seq_len, state_re, state_im, conv_state, mem_k, mem_v, s5p, glu_w, glu_b, conv_w, conv_b,
                  i, layer, nh_mem, cols, *, decode):
    m = z.shape[0]
    n = m // seq_len
    w, c = glu_w.shape[1], conv_w.shape[2]
    o_ua, o_ga, o_bg, o_cg, o_hb, o_gb, o_qm, o_gm = cols
    sps = seq_len if decode else seq_len // SUBLANES
    if decode:
        to_scan = lambda a: a.reshape(n, seq_len, -1).transpose(1, 0, 2)
        u = to_scan(z[:, o_ua:o_ua + w]).reshape(1, m, w)
        h0_re, h0_im = state_re.reshape(1, n, -1), state_im.reshape(1, n, -1)
    else:
        u = z[:, o_ua:o_ua + w].reshape(n, SUBLANES, sps, w).transpose(0, 2, 1, 3).reshape(n, seq_len, w)
        pad = lambda s: jnp.pad(s.reshape(n, 1, -1), ((0, 0), (0, SUBLANES - 1), (0, 0)))
        h0_re, h0_im = pad(state_re), pad(state_im)
    y, h_re, h_im = _s5(u, h0_re, h0_im, s5p, chain=not decode)
    if decode:
        y = y.reshape(seq_len, n, w).transpose(1, 0, 2).reshape(m, w)
        h_re, h_im = h_re[0], h_im[0]
    else:
        y = y.reshape(n, sps, SUBLANES, w).transpose(0, 2, 1, 3).reshape(m, w)
        h_re, h_im = h_re[:, SUBLANES - 1], h_im[:, SUBLANES - 1]
    mix_dtype = F32 if decode else BF16
    mix_a = _glu_gate(y, z, o_ga, glu_w, glu_b, i, tm=1024, tn=512, out_dtype=mix_dtype)
    if decode:
        bg, cg, hb, gb = (to_scan(z[:, o:o + c]) for o in (o_bg, o_cg, o_hb, o_gb))
        mix_b, conv_new = _conv_gate_step(bg, cg, hb, gb, conv_state.transpose(1, 0, 2), conv_w, conv_b, i, ct=512)
        mix_b = mix_b.transpose(1, 0, 2).reshape(m, c)
        conv_new = conv_new.transpose(1, 0, 2)
    else:
        mix_b, conv_new = _conv_gate(z, (o_bg, o_cg, o_hb, o_gb), c, conv_state, conv_w, conv_b, i, seq_len,
                                     tm=512, ct=512)
    mix_m = _mem_attend(z, o_qm, o_gm, mem_k, mem_v, layer, seq_len, nh_mem, tq=512, joint=decode,
                        out_dtype=mix_dtype)
    gp = state_re.shape[1:]
    return (mix_a, mix_b, mix_m), (h_re.reshape((n,) + gp), h_im.reshape((n,) + gp), conv_new)


def _ssm_conv_layer(xp, xs, t_p, t_s, states_s, mem_p, mem_s, w_in, w_out, s5p, glu_w, glu_b, conv_w, conv_b,
                    ln_g, ln_b, i, layer, alpha, nh_mem):
    d = xp[0].shape[1]
    n_p = xp[0].shape[0] // t_p
    w, c, mw = glu_w.shape[1], conv_w.shape[2], mem_p[0].shape[2]
    cols, n_cols = _offsets((w, w, c, c, c, c, mw, mw))
    z_p, z_s = _matmul([(xp[1], xs[1], 0, w_in, i, 0, 0, d)], n_cols, tm=1024, tn=512, name="in_proj_ssm")
    zero_state = jnp.zeros((n_p,) + states_s[0].shape[1:], F32)
    zero_conv = jnp.zeros((n_p,) + states_s[2].shape[1:], F32)
    branch = (s5p, glu_w, glu_b, conv_w, conv_b, i, layer, nh_mem, cols)
    mix_p, st_p = _ssm_conv_mix(z_p, t_p, zero_state, zero_state, zero_conv, *mem_p, *branch, decode=False)
    mix_s, st_s = _ssm_conv_mix(z_s, t_s, *states_s, *mem_s, *branch, decode=True)
    out_proj = lambda mix, x: _matmul([(mix[0], None, 0, w_out, i, 0, 0, w), (mix[1], None, 0, w_out, i, w // c, 0, c),
                                       (mix[2], None, 0, w_out, i, (w + c) // mw, 0, mw)],
                                      d, tm=1024, tn=256, res=x[0], alpha=alpha, name="out_proj_ssm")
    y_p, y_s = out_proj(mix_p, xp), out_proj(mix_s, xs)
    return _layer_norm(y_p, ln_g, ln_b, layer, tm=256), _layer_norm(y_s, ln_g, ln_b, layer, tm=256), st_p, st_s


def _fox_layer(xp, xs, t_p, t_s, past, mem_p, mem_s, w_in_t, w_tail_t, b_f128, w_out, ln_g, ln_b, i, layer, alpha,
               nh, nh_mem):
    d = xp[0].shape[1]
    m_p, m_s = xp[0].shape[0], xs[0].shape[0]
    n_p, n_s = m_p // t_p, m_s // t_s
    mw = mem_p[0].shape[2]
    fw = w_out.shape[1] - mw
    hd = fw // nh
    proj = [_matmul([(xp[1], xs[1], 0, w_in_t, i, 0, o * fw, d)], fw, tm=1024, tn=512, transposed=True,
                    name="in_proj_fox") for o in range(4)]
    (zq_p, zq_s), (zk_p, zk_s), (zv_p, zv_s), (zg_p, zg_s) = proj
    tail_p, tail_s = _matmul([(xp[1], xs[1], 0, w_tail_t, i, 0, 0, d)], w_tail_t.shape[1], tm=1024, tn=384,
                             transposed=True, name="in_proj_fox_tail")
    o_qm, o_gm, o_f = 0, mw, 2 * mw

    tq = tk = min(512, t_p)
    lf_p, c, ck3, r = _forget_gates(tail_p, o_f, b_f128, i, n_p, t_p, tk=tk)
    mix_f_p = _fox_attend(zq_p, zk_p, zv_p, zg_p, c, ck3, r, n_p, t_p, nh, hd, tq=tq, tk=tk, hps=4)
    mix_m_p = _mem_attend(tail_p, o_qm, o_gm, *mem_p, layer, t_p, nh_mem, tq=512, joint=False, out_dtype=BF16)

    kc, vc, after, tot, page_table = past
    ng = nh // SUBLANES
    lf_s, c_new = _forget_gates_step(tail_s, o_f, b_f128, i, t_s)
    grouped = lambda a: (a.reshape(n_s, t_s, ng, SUBLANES, -1).transpose(0, 2, 1, 3, 4)
                         .reshape(n_s, ng, t_s * SUBLANES, -1))
    c_new = grouped(c_new[:, :nh]).reshape(n_s, ng, 1, t_s * SUBLANES)
    y = _fox_attend_step(grouped(zq_s), grouped(zk_s), grouped(zv_s), c_new, kc, vc, after, tot, page_table, i, t_s,
                         pps=2)
    y = y.reshape(n_s, ng, t_s, SUBLANES, hd).transpose(0, 2, 1, 3, 4).reshape(m_s, fw)
    mix_f_s = _gate_step(y, zg_s)
    mix_m_s = _mem_attend(tail_s, o_qm, o_gm, *mem_s, layer, t_s, nh_mem, tq=512, joint=True, out_dtype=F32)

    out_proj = lambda mix_f, mix_m, x: _matmul([(mix_f, None, 0, w_out, i, 0, 0, fw),
                                                (mix_m, None, 0, w_out, i, fw // mw, 0, mw)],
                                               d, tm=1024, tn=256, res=x[0], alpha=alpha, name="out_proj_fox")
    y_p, y_s = out_proj(mix_f_p, mix_m_p, xp), out_proj(mix_f_s, mix_m_s, xs)
    kv_p = (zk_p.reshape(n_p, t_p, nh, hd), zv_p.reshape(n_p, t_p, nh, hd), lf_p[:, :, :nh])
    kv_s = (zk_s.reshape(n_s, t_s, nh, hd), zv_s.reshape(n_s, t_s, nh, hd), lf_s[:, :nh].reshape(n_s, t_s, nh))
    return _layer_norm(y_p, ln_g, ln_b, layer, tm=256), _layer_norm(y_s, ln_g, ln_b, layer, tm=256), kv_p, kv_s


def _gate_step_body(y_ref, g_ref, o_ref):
    o_ref[...] = y_ref[...] * _silu(g_ref[...])


def _gate_step(y, g):
    m, w = y.shape
    blk = pl.BlockSpec((m, w), lambda i: (0, 0))
    return pl.pallas_call(_gate_step_body, grid=(1,), in_specs=[blk, blk], out_specs=blk,
                          out_shape=jax.ShapeDtypeStruct((m, w), F32), compiler_params=_params("arbitrary"),
                          name="gate_step")(y, g)


def kernel(x_prompt, x_sample, cache_fox_k, cache_fox_v, cache_fox_lf, cache_mem_k, cache_mem_v, state_ssm_re, state_ssm_im, state_conv, page_table, mem_prompt, w_in_ssm, w_out_ssm, ssm_lam_re, ssm_lam_im, ssm_log_dt, ssm_b_re, ssm_b_im, ssm_c_re, ssm_c_im, ssm_d, glu_w, glu_b, conv_w, conv_b, w_in_fox, b_fgate, w_out_fox, w_mem_k, w_mem_v, ln_g, ln_b):
    n_p, t_p, d = x_prompt.shape
    n_s, t_s, _ = x_sample.shape
    depth = ln_g.shape[0]
    alpha = (2 * depth) ** 0.25
    nh_fox = b_fgate.shape[1]
    nh_mem = cache_mem_k.shape[3]
    mw = w_mem_k.shape[2]
    mem_len = mem_prompt.shape[1]
    fw = w_out_fox.shape[1] - mw
    n_fox, n_pool, ps = cache_fox_k.shape[:3]

    w_in_t = w_in_fox.transpose(0, 2, 1)
    w_tail_t = jnp.concatenate([w_in_t[:, 4 * fw + nh_fox:], w_in_t[:, 4 * fw:4 * fw + nh_fox]], axis=1)
    w_tail_t = jnp.pad(w_tail_t, ((0, 0), (0, LANES - nh_fox), (0, 0)))
    b_f128 = jnp.pad(b_fgate.astype(F32), ((0, 0), (0, LANES - nh_fox))).reshape(n_fox, 1, LANES)
    assert nh_fox % SUBLANES == 0
    ng = nh_fox // SUBLANES
    kc = cache_fox_k.reshape(n_fox, n_pool, ps, ng, SUBLANES, fw // nh_fox)
    vc = cache_fox_v.reshape(n_fox, n_pool, ps, ng, SUBLANES, fw // nh_fox)
    lfc = (cache_fox_lf.astype(F32).reshape(n_fox, n_pool, ps, ng, SUBLANES).transpose(0, 1, 3, 2, 4)
           .reshape(n_fox, n_pool * ng, ps * SUBLANES))
    lf_after, lf_tot = (a.reshape(n_fox, n_pool, ng, ps * SUBLANES) for a in _page_sums(lfc, rows=256))
    past = (kc, vc, lf_after, lf_tot, page_table)
    mem_s = (cache_mem_k.reshape(depth, n_s * mem_len, mw), cache_mem_v.reshape(depth, n_s * mem_len, mw))

    mem_in = mem_prompt.reshape(n_p * mem_len, d).astype(BF16)
    mem_p = (_memproj(mem_in, w_mem_k), _memproj(mem_in, w_mem_v))

    xp = (x_prompt.reshape(n_p * t_p, d),)
    xs = (x_sample.reshape(n_s * t_s, d),)
    xp, xs = xp + (xp[0].astype(BF16),), xs + (xs[0].astype(BF16),)

    fox_p, fox_s, ssm_p, ssm_s = [], [], [], []
    for layer in range(depth):
        i = layer // 2
        if layer % 2 == 0:
            s5p = _s5_discretise(ssm_lam_re[i], ssm_lam_im[i], ssm_log_dt[i], ssm_b_re[i], ssm_b_im[i],
                                 ssm_c_re[i], ssm_c_im[i], ssm_d[i])
            states_s = (state_ssm_re[i].astype(F32), state_ssm_im[i].astype(F32), state_conv[i].astype(F32))
            xp, xs, st_p, st_s = _ssm_conv_layer(xp, xs, t_p, t_s, states_s, mem_p, mem_s, w_in_ssm, w_out_ssm, s5p,
                                                 glu_w, glu_b, conv_w, conv_b, ln_g, ln_b, i, layer, alpha, nh_mem)
            ssm_p.append(st_p), ssm_s.append(st_s)
        else:
            xp, xs, kv_p, kv_s = _fox_layer(xp, xs, t_p, t_s, past, mem_p, mem_s, w_in_t, w_tail_t, b_f128,
                                            w_out_fox, ln_g, ln_b, i, layer, alpha, nh_fox, nh_mem)
            fox_p.append(kv_p), fox_s.append(kv_s)
    stack = lambda rows, k: jnp.stack([r[k] for r in rows])
    mem_shape = (depth, n_p, mem_len, nh_mem, mw // nh_mem)
    return (xp[0].reshape(n_p, t_p, d), xs[0].reshape(n_s, t_s, d),
            stack(fox_p, 0), stack(fox_p, 1), stack(fox_p, 2),
            stack(fox_s, 0), stack(fox_s, 1), stack(fox_s, 2),
            stack(ssm_p, 0), stack(ssm_p, 1), stack(ssm_p, 2),
            stack(ssm_s, 0), stack(ssm_s, 1), stack(ssm_s, 2),
            mem_p[0].reshape(mem_shape), mem_p[1].reshape(mem_shape))
```

```python
from functools import partial

import jax
import jax.numpy as jnp
from jax import lax
from jax.experimental import pallas as pl
from jax.experimental.pallas import tpu as pltpu

F32 = jnp.float32
BF16 = jnp.bfloat16

LN_EPS = 1e-5
LANES = 128
SUBLANES = 8
VMEM_LIMIT_BYTES = 56 * 1024 * 1024
NT_DIMS = (((1,), (1,)), ((), ()))
LOG2E = 1.4426950408889634


def _params(*sem):
    return pltpu.CompilerParams(dimension_semantics=sem, vmem_limit_bytes=VMEM_LIMIT_BYTES)


def _silu(g):
    return g * jax.nn.sigmoid(g)


def _mm_body(*refs, n_pairs, transposed, alpha, has_res, has_side):
    per = 3 if has_side else 2
    n_in = per * n_pairs + (per - 1 if has_res else 0)
    out_ref = refs[n_in]

    def product(x_of_pair, w_of_pair, res_ref):
        acc = None
        for p in range(n_pairs):
            x = x_of_pair(p)[...].astype(BF16)
            if transposed:
                d = lax.dot_general(x, w_of_pair(p), NT_DIMS, preferred_element_type=F32)
            else:
                d = jnp.dot(x, w_of_pair(p), preferred_element_type=F32)
            acc = d if acc is None else acc + d
        return acc if res_ref is None else alpha * res_ref[...] + acc

    res_ref = refs[per * n_pairs] if has_res else None
    if has_side:
        wb = refs[n_in + 2:]

        @pl.when(pl.program_id(1) == 0)
        def _():
            for p in range(n_pairs):
                wb[p][...] = refs[3 * p + 1][...].astype(BF16)
            refs[n_in + 1][...] = product(lambda p: refs[3 * p + 2], lambda p: wb[p][...],
                                          refs[3 * n_pairs + 1] if has_res else None)

        out_ref[...] = product(lambda p: refs[3 * p], lambda p: wb[p][...], res_ref).astype(out_ref.dtype)
    else:
        out_ref[...] = product(lambda p: refs[2 * p], lambda p: refs[2 * p + 1][...].astype(BF16),
                               res_ref).astype(out_ref.dtype)


def _matmul(pairs, n_out, *, tm, tn, res=None, alpha=1.0, out_dtype=F32, transposed=False, name="mm"):
    has_side = pairs[0][1] is not None
    m = pairs[0][0].shape[0]
    tm, tn = min(tm, m), min(tn, n_out)
    assert m % tm == 0 and n_out % tn == 0
    rc = (lambda f: (lambda j, i: f(i, j))) if has_side else (lambda f: f)
    in_specs, args, scratch = [], [], []
    for (x, xs, xkb, w, layer, wkb, col0, kdim) in pairs:
        assert col0 % tn == 0
        cb = col0 // tn
        in_specs.append(pl.BlockSpec((tm, kdim), rc(lambda i, j, xkb=xkb: (i, xkb))))
        if transposed:
            in_specs.append(pl.BlockSpec((None, tn, kdim), rc(lambda i, j, l=layer, wkb=wkb, cb=cb: (l, cb + j, wkb))))
            scratch.append(pltpu.VMEM((tn, kdim), BF16))
        else:
            in_specs.append(pl.BlockSpec((None, kdim, tn), rc(lambda i, j, l=layer, wkb=wkb, cb=cb: (l, wkb, cb + j))))
            scratch.append(pltpu.VMEM((kdim, tn), BF16))
        args += [x, w]
        if has_side:
            in_specs.append(pl.BlockSpec((xs.shape[0], kdim), rc(lambda i, j, xkb=xkb: (0, xkb))))
            args.append(xs)
    out_specs = [pl.BlockSpec((tm, tn), rc(lambda i, j: (i, j)))]
    out_shape = [jax.ShapeDtypeStruct((m, n_out), out_dtype)]
    if res is not None:
        in_specs.append(pl.BlockSpec((tm, tn), rc(lambda i, j: (i, j))))
        args.append(res[0] if has_side else res)
    if has_side:
        ms = pairs[0][1].shape[0]
        if res is not None:
            in_specs.append(pl.BlockSpec((ms, tn), rc(lambda i, j: (0, j))))
            args.append(res[1])
        out_specs.append(pl.BlockSpec((ms, tn), rc(lambda i, j: (0, j))))
        out_shape.append(jax.ShapeDtypeStruct((ms, n_out), F32))
    out = pl.pallas_call(
        partial(_mm_body, n_pairs=len(pairs), transposed=transposed, alpha=alpha, has_res=res is not None,
                has_side=has_side),
        grid=(n_out // tn, m // tm) if has_side else (m // tm, n_out // tn),
        in_specs=in_specs,
        out_specs=out_specs,
        out_shape=out_shape,
        scratch_shapes=scratch if has_side else [],
        compiler_params=_params("parallel", "arbitrary"),
        name=name,
    )(*args)
    return out if has_side else out[0]


def _memproj_body(x_ref, w_ref, o_ref):
    o_ref[...] = jnp.dot(x_ref[...].astype(BF16), w_ref[...].astype(BF16), preferred_element_type=F32)


def _memproj(x, w):
    nl, d, n = w.shape
    r = x.shape[0]
    return pl.pallas_call(
        _memproj_body,
        grid=(nl,),
        in_specs=[pl.BlockSpec((r, d), lambda l: (0, 0)), pl.BlockSpec((None, d, n), lambda l: (l, 0, 0))],
        out_specs=pl.BlockSpec((None, r, n), lambda l: (l, 0, 0)),
        out_shape=jax.ShapeDtypeStruct((nl, r, n), F32),
        compiler_params=_params("arbitrary"),
        name="memproj",
    )(x, w)


def _ln_body(y_ref, g_ref, b_ref, xf_ref, xb_ref):
    y = y_ref[...]
    mu = jnp.mean(y, axis=-1, keepdims=True)
    d = y - mu
    var = jnp.mean(d * d, axis=-1, keepdims=True)
    out = d * lax.rsqrt(var + LN_EPS) * g_ref[...] + b_ref[...]
    xf_ref[...] = out
    xb_ref[...] = out.astype(BF16)


def _layer_norm(y, g, b, layer, *, tm):
    m, d = y.shape
    tm = min(tm, m)
    g3 = g.reshape(g.shape[0], 1, d)
    b3 = b.reshape(b.shape[0], 1, d)
    vec = pl.BlockSpec((None, 1, d), lambda i: (layer, 0, 0))
    row = pl.BlockSpec((tm, d), lambda i: (i, 0))
    return pl.pallas_call(
        _ln_body,
        grid=(m // tm,),
        in_specs=[row, vec, vec],
        out_specs=[row, row],
        out_shape=[jax.ShapeDtypeStruct((m, d), F32), jax.ShapeDtypeStruct((m, d), BF16)],
        compiler_params=_params("parallel"),
        name="layernorm",
    )(y, g3, b3)


def _s5_discretise(lam_re, lam_im, log_dt, b_re, b_im, c_re, c_im, d_skip):
    g, p, h = b_re.shape
    gpc = LANES // h
    ncc = g // gpc
    lam_re, lam_im = lam_re.astype(F32), lam_im.astype(F32)
    dt = jnp.exp(log_dt.astype(F32))[:, None]
    mag = jnp.exp(lam_re * dt)
    ar = mag * jnp.cos(lam_im * dt)
    ai = mag * jnp.sin(lam_im * dt)
    nr = ar - 1.0
    den = jnp.square(lam_re) + jnp.square(lam_im)
    cr = ((nr * lam_re + ai * lam_im) / den)[..., None]
    ci = ((ai * lam_re - nr * lam_im) / den)[..., None]
    b_re, b_im = b_re.astype(F32), b_im.astype(F32)
    bb_re = cr * b_re - ci * b_im
    bb_im = cr * b_im + ci * b_re
    eye = jnp.eye(gpc, dtype=F32)

    def pack_in(bb):
        return jnp.einsum('cgph,gk->cghkp', bb.reshape(ncc, gpc, p, h), eye).reshape(ncc, gpc * h, gpc * p)

    def pack_out(c):
        return jnp.einsum('cghp,gk->cgpkh', c.astype(F32).reshape(ncc, gpc, h, p), eye).reshape(ncc, gpc * p, gpc * h)

    return dict(a_re=ar.reshape(ncc, 1, gpc * p), a_im=ai.reshape(ncc, 1, gpc * p),
                wb_re=pack_in(bb_re), wb_im=pack_in(bb_im), wc_re=pack_out(c_re), wc_im=pack_out(c_im),
                d=d_skip.astype(F32).reshape(ncc, 1, gpc * h))


def _s5_body(u_ref, h0r_ref, h0i_ref, ar_ref, ai_ref, wbr_ref, wbi_ref, wcr_ref, wci_ref, d_ref,
             y_ref, hr_out, hi_out, hre, him, *, steps, chain):
    u = u_ref[...]
    ub = u.astype(BF16)
    hre[...] = jnp.dot(ub, wbr_ref[...].astype(BF16), preferred_element_type=F32)
    him[...] = jnp.dot(ub, wbi_ref[...].astype(BF16), preferred_element_type=F32)
    ns = hre.shape[1]
    ar = jnp.broadcast_to(ar_ref[...], (SUBLANES, ns))
    ai = jnp.broadcast_to(ai_ref[...], (SUBLANES, ns))

    def rows_of(t):
        return pl.ds(pl.multiple_of(t * SUBLANES, SUBLANES), SUBLANES)

    def scan_step(t, c):
        hr, hi = c
        rows = rows_of(t)
        nr = ar * hr - ai * hi + hre[rows, :]
        ni = ar * hi + ai * hr + him[rows, :]
        hre[rows, :] = nr
        him[rows, :] = ni
        return nr, ni

    fr, fi = lax.fori_loop(0, steps, scan_step, (h0r_ref[...], h0i_ref[...]), unroll=2)

    if chain:
        alr, ali = None, None
        sqr, sqi, e = ar, ai, steps
        while e:
            if e & 1:
                alr, ali = (sqr, sqi) if alr is None else (alr * sqr - ali * sqi, alr * sqi + ali * sqr)
            sqr, sqi = sqr * sqr - sqi * sqi, 2.0 * sqr * sqi
            e >>= 1
        sub = lax.broadcasted_iota(jnp.int32, (SUBLANES, ns), 0)
        sr, si = fr, fi
        for c in range(1, SUBLANES):
            pr, pi = pltpu.roll(sr, 1, 0), pltpu.roll(si, 1, 0)
            sr = jnp.where(sub == c, alr * pr - ali * pi + fr, sr)
            si = jnp.where(sub == c, alr * pi + ali * pr + fi, si)
        cin_r = jnp.where(sub == 0, 0.0, pltpu.roll(sr, 1, 0))
        cin_i = jnp.where(sub == 0, 0.0, pltpu.roll(si, 1, 0))

        def fix_step(t, c):
            gr, gi = c
            rows = rows_of(t)
            nr, ni = ar * gr - ai * gi, ar * gi + ai * gr
            hre[rows, :] = hre[rows, :] + nr
            him[rows, :] = him[rows, :] + ni
            return nr, ni

        lax.fori_loop(0, steps, fix_step, (cin_r, cin_i), unroll=2)
        fr, fi = sr, si

    hr_out[...] = fr
    hi_out[...] = fi
    y = (jnp.dot(hre[...].astype(BF16), wcr_ref[...].astype(BF16), preferred_element_type=F32)
         - jnp.dot(him[...].astype(BF16), wci_ref[...].astype(BF16), preferred_element_type=F32)
         + d_ref[...] * u)
    y_ref[...] = y


def _s5(u, h0_re, h0_im, prm, *, chain):
    nb, rows, w = u.shape
    steps = rows // SUBLANES
    ncc, cw, ns = prm["wb_re"].shape
    assert w == ncc * cw and cw == LANES
    seq = pl.BlockSpec((None, rows, cw), lambda b, c: (b, 0, c))
    st = pl.BlockSpec((None, SUBLANES, ns), lambda b, c: (b, 0, c))

    def par(shape):
        return pl.BlockSpec((None,) + shape, lambda b, c: (c, 0, 0))

    return pl.pallas_call(
        partial(_s5_body, steps=steps, chain=chain),
        grid=(nb, ncc),
        in_specs=[seq, st, st, par((1, ns)), par((1, ns)), par((cw, ns)), par((cw, ns)),
                  par((ns, cw)), par((ns, cw)), par((1, cw))],
        out_specs=[seq, st, st],
        out_shape=[jax.ShapeDtypeStruct((nb, rows, w), F32),
                   jax.ShapeDtypeStruct((nb, SUBLANES, ncc * ns), F32),
                   jax.ShapeDtypeStruct((nb, SUBLANES, ncc * ns), F32)],
        scratch_shapes=[pltpu.VMEM((rows, ns), F32), pltpu.VMEM((rows, ns), F32)],
        compiler_params=_params("parallel", "arbitrary"),
        name="s5_scan",
    )(u, h0_re, h0_im, prm["a_re"], prm["a_im"], prm["wb_re"], prm["wb_im"], prm["wc_re"], prm["wc_im"], prm["d"])


def _glu_body(yfull_ref, yblk_ref, ga_ref, w_ref, b_ref, o_ref, zb):
    @pl.when(pl.program_id(1) == 0)
    def _():
        zb[...] = jax.nn.gelu(yfull_ref[...]).astype(BF16)

    glu = jnp.dot(zb[...], w_ref[...].astype(BF16), preferred_element_type=F32) + b_ref[...]
    z = jax.nn.gelu(yblk_ref[...])
    o_ref[...] = (z * jax.nn.sigmoid(glu) * _silu(ga_ref[...])).astype(o_ref.dtype)


def _glu_gate(y, z_all, ga_col0, glu_w, glu_b, layer, *, tm, tn, out_dtype):
    m, w = y.shape
    tm, tn = min(tm, m), min(tn, w)
    assert ga_col0 % tn == 0
    gb3 = glu_b.reshape(glu_b.shape[0], 1, w)
    return pl.pallas_call(
        _glu_body,
        grid=(m // tm, w // tn),
        in_specs=[pl.BlockSpec((tm, w), lambda i, j: (i, 0)),
                  pl.BlockSpec((tm, tn), lambda i, j: (i, j)),
                  pl.BlockSpec((tm, tn), lambda i, j: (i, ga_col0 // tn + j)),
                  pl.BlockSpec((None, w, tn), lambda i, j: (layer, 0, j)),
                  pl.BlockSpec((None, 1, tn), lambda i, j: (layer, 0, j))],
        out_specs=pl.BlockSpec((tm, tn), lambda i, j: (i, j)),
        out_shape=jax.ShapeDtypeStruct((m, w), out_dtype),
        scratch_shapes=[pltpu.VMEM((tm, w), BF16)],
        compiler_params=_params("parallel", "arbitrary"),
        name="glu_gate",
    )(y, y, z_all, glu_w, gb3)


def _conv_body(bg_ref, cg_ref, hb_ref, gb_ref, cs_ref, w_ref, b_ref, o_ref, cn_ref, prev, *, tiles_per_seq):
    @pl.when(pl.program_id(1) % tiles_per_seq == 0)
    def _():
        prev[...] = cs_ref[...]

    v = cg_ref[...] * hb_ref[...]
    tm = v.shape[0]
    p2, p1 = prev[0:1, :], prev[1:2, :]
    row = lax.broadcasted_iota(jnp.int32, v.shape, 0)
    v1 = jnp.where(row == 0, p1, pltpu.roll(v, 1, 0))
    v2 = jnp.where(row == 0, p2, jnp.where(row == 1, p1, pltpu.roll(v, 2, 0)))
    y = b_ref[...] + w_ref[0:1, :] * v2 + w_ref[1:2, :] * v1 + w_ref[2:3, :] * v
    o_ref[...] = (bg_ref[...] * y * _silu(gb_ref[...])).astype(o_ref.dtype)
    last = v[tm - 2:tm, :]
    prev[...] = last
    cn_ref[...] = last


def _conv_gate(z_all, col0, c, conv_state, conv_w, conv_b, layer, seq_len, *, tm, ct):
    m = z_all.shape[0]
    tm, ct = min(tm, seq_len), min(ct, c)
    assert seq_len % tm == 0 and all(o % ct == 0 for o in col0) and conv_w.shape[1] == 3
    tps = seq_len // tm
    n = m // seq_len
    cb3 = conv_b.reshape(conv_b.shape[0], 1, c)

    def zcol(o):
        return pl.BlockSpec((tm, ct), lambda j, i, o=o: (i, o // ct + j))

    st = pl.BlockSpec((None, 2, ct), lambda j, i: (i // tps, 0, j))
    return pl.pallas_call(
        partial(_conv_body, tiles_per_seq=tps),
        grid=(c // ct, m // tm),
        in_specs=[zcol(col0[0]), zcol(col0[1]), zcol(col0[2]), zcol(col0[3]), st,
                  pl.BlockSpec((None, 3, ct), lambda j, i: (layer, 0, j)),
                  pl.BlockSpec((None, 1, ct), lambda j, i: (layer, 0, j))],
        out_specs=[pl.BlockSpec((tm, ct), lambda j, i: (i, j)), st],
        out_shape=[jax.ShapeDtypeStruct((m, c), BF16), jax.ShapeDtypeStruct((n, 2, c), F32)],
        scratch_shapes=[pltpu.VMEM((2, ct), F32)],
        compiler_params=_params("parallel", "arbitrary"),
        name="conv_gate",
    )(z_all, z_all, z_all, z_all, conv_state, conv_w, cb3)


def _conv_step_body(bg_ref, cg_ref, hb_ref, gb_ref, cs_ref, w_ref, b_ref, o_ref, cn_ref, *, steps):
    vp = [cs_ref[0], cs_ref[1]] + [cg_ref[t] * hb_ref[t] for t in range(steps)]
    for t in range(steps):
        y = b_ref[...] + w_ref[0:1, :] * vp[t] + w_ref[1:2, :] * vp[t + 1] + w_ref[2:3, :] * vp[t + 2]
        o_ref[t] = bg_ref[t] * y * _silu(gb_ref[t])
    cn_ref[0] = vp[steps]
    cn_ref[1] = vp[steps + 1]


def _conv_gate_step(bg, cg, hb, gb, conv_state_t, conv_w, conv_b, layer, *, ct):
    steps, n, c = bg.shape
    ct = min(ct, c)
    assert conv_w.shape[1] == 3
    cb3 = conv_b.reshape(conv_b.shape[0], 1, c)
    blk = pl.BlockSpec((steps, n, ct), lambda j: (0, 0, j))
    st = pl.BlockSpec((2, n, ct), lambda j: (0, 0, j))
    return pl.pallas_call(
        partial(_conv_step_body, steps=steps),
        grid=(c // ct,),
        in_specs=[blk, blk, blk, blk, st,
                  pl.BlockSpec((None, 3, ct), lambda j: (layer, 0, j)),
                  pl.BlockSpec((None, 1, ct), lambda j: (layer, 0, j))],
        out_specs=[blk, st],
        out_shape=[jax.ShapeDtypeStruct((steps, n, c), F32), jax.ShapeDtypeStruct((2, n, c), F32)],
        compiler_params=_params("parallel"),
        name="conv_gate_step",
    )(bg, cg, hb, gb, conv_state_t, conv_w, cb3)


def _mem_body(q_ref, g_ref, k_ref, v_ref, o_ref, *, nh, hd, rows_per_seq, keys_per_seq):
    scale = hd ** -0.5
    tq, nk = q_ref.shape[0], k_ref.shape[0]
    if rows_per_seq is not None:
        rseq = lax.broadcasted_iota(jnp.int32, (tq, nk), 0) // rows_per_seq
        kseq = lax.broadcasted_iota(jnp.int32, (tq, nk), 1) // keys_per_seq
        own = rseq == kseq
    for h in range(nh):
        sl = slice(h * hd, (h + 1) * hd)
        q = q_ref[:, sl].astype(BF16)
        k = k_ref[:, sl].astype(BF16)
        v = v_ref[:, sl].astype(BF16)
        s = lax.dot_general(q, k, NT_DIMS, preferred_element_type=F32) * scale
        if rows_per_seq is not None:
            s = jnp.where(own, s, -jnp.inf)
        e = jnp.exp(s - jnp.max(s, axis=-1, keepdims=True))
        p = e / jnp.sum(e, axis=-1, keepdims=True)
        o = jnp.dot(p.astype(BF16), v, preferred_element_type=F32)
        o_ref[:, sl] = (o * _silu(g_ref[:, sl])).astype(o_ref.dtype)


def _mem_attend(z_all, q_col0, g_col0, mem_k, mem_v, layer, seq_len, nh, *, tq, joint, out_dtype):
    m = z_all.shape[0]
    mw = mem_k.shape[2]
    n = m // seq_len
    mem_len = mem_k.shape[1] // n
    assert q_col0 % mw == 0 and g_col0 % mw == 0
    if joint:
        tq, kv_rows, rps = m, n * mem_len, seq_len
        kv_idx = lambda i: (layer, 0, 0)
    else:
        tq = min(tq, seq_len)
        assert seq_len % tq == 0
        tps = seq_len // tq
        kv_rows, rps = mem_len, None
        kv_idx = lambda i: (layer, i // tps, 0)
    kv = pl.BlockSpec((None, kv_rows, mw), kv_idx)
    return pl.pallas_call(
        partial(_mem_body, nh=nh, hd=mw // nh, rows_per_seq=rps, keys_per_seq=mem_len),
        grid=(m // tq,),
        in_specs=[pl.BlockSpec((tq, mw), lambda i: (i, q_col0 // mw)),
                  pl.BlockSpec((tq, mw), lambda i: (i, g_col0 // mw)), kv, kv],
        out_specs=pl.BlockSpec((tq, mw), lambda i: (i, 0)),
        out_shape=jax.ShapeDtypeStruct((m, mw), out_dtype),
        compiler_params=_params("parallel"),
        name="mem_attend",
    )(z_all, z_all, mem_k, mem_v)


def _log_sigmoid(x):
    return jnp.minimum(x, 0.0) - jnp.log1p(jnp.exp(-jnp.abs(x)))


def _split3(x):
    hi = x.astype(BF16)
    r = x - hi.astype(F32)
    mid = r.astype(BF16)
    return hi, mid, (r - mid.astype(F32)).astype(BF16)


def _lf_body(f_ref, bf_ref, lf_ref, c_ref, ck3_ref, r_ref, *, tk):
    lf = _log_sigmoid(f_ref[...] + bf_ref[...])
    lf_ref[...] = lf
    t = lf.shape[0]
    row = lax.broadcasted_iota(jnp.int32, lf.shape, 0)
    c, s = lf, 1
    while s < t:
        c = c + jnp.where(row >= s, pltpu.roll(c, s, 0), 0.0)
        s *= 2
    c_ref[...] = c
    for j in range(t // tk):
        rows = slice(j * tk, (j + 1) * tk)
        r = c[(j + 1) * tk - 1:(j + 1) * tk, :]
        r_ref[j:j + 1, :] = r
        for part, term in enumerate(_split3((r - c[rows, :]) * LOG2E)):
            ck3_ref[rows, part * LANES:(part + 1) * LANES] = term


def _forget_gates(z_tail, f_col0, b_f128, layer, n, seq_len, *, tk):
    assert f_col0 % LANES == 0 and seq_len % tk == 0
    z3 = z_tail.reshape(n, seq_len, z_tail.shape[1])
    blk = pl.BlockSpec((None, seq_len, LANES), lambda b: (b, 0, 0))
    return pl.pallas_call(
        partial(_lf_body, tk=tk),
        grid=(n,),
        in_specs=[pl.BlockSpec((None, seq_len, LANES), lambda b: (b, 0, f_col0 // LANES)),
                  pl.BlockSpec((None, 1, LANES), lambda b: (layer, 0, 0))],
        out_specs=[blk, blk, pl.BlockSpec((None, seq_len, 3 * LANES), lambda b: (b, 0, 0)),
                   pl.BlockSpec((None, seq_len // tk, LANES), lambda b: (b, 0, 0))],
        out_shape=[jax.ShapeDtypeStruct((n, seq_len, LANES), F32), jax.ShapeDtypeStruct((n, seq_len, LANES), F32),
                   jax.ShapeDtypeStruct((n, seq_len, 3 * LANES), BF16),
                   jax.ShapeDtypeStruct((n, seq_len // tk, LANES), F32)],
        compiler_params=_params("parallel"),
        name="forget_gates",
    )(z3, b_f128)


def _lf_step_body(f_ref, bf_ref, lf_ref, c_ref, *, steps):
    lf = _log_sigmoid(f_ref[...] + bf_ref[...])
    lf_ref[...] = lf
    step = lax.broadcasted_iota(jnp.int32, lf.shape, 0) % steps
    c, s = lf, 1
    while s < steps:
        c = c + jnp.where(step >= s, pltpu.roll(c, s, 0), 0.0)
        s *= 2
    c_ref[...] = c


def _forget_gates_step(z_tail, f_col0, b_f128, layer, steps):
    m = z_tail.shape[0]
    out = pl.BlockSpec((m, LANES), lambda i: (0, 0))
    return pl.pallas_call(
        partial(_lf_step_body, steps=steps),
        grid=(1,),
        in_specs=[pl.BlockSpec((m, LANES), lambda i: (0, f_col0 // LANES)),
                  pl.BlockSpec((None, 1, LANES), lambda i: (layer, 0, 0))],
        out_specs=[out, out],
        out_shape=[jax.ShapeDtypeStruct((m, LANES), F32), jax.ShapeDtypeStruct((m, LANES), F32)],
        compiler_params=_params("arbitrary"),
        name="forget_gates_step",
    )(z_tail, b_f128)


def _fox_body(q_ref, k_ref, v_ref, g_ref, c_ref, ck3_ref, r_ref, o_ref, ka, va, m_s, acc, *, tq, tk, hd, hps):
    h0 = pl.program_id(1) * hps
    qi = pl.program_id(2)
    seq_len = k_ref.shape[0]
    cols = lambda u: slice(u * hd, (u + 1) * hd)

    @pl.when(qi == 0)
    def _():
        sel_r = lax.broadcasted_iota(jnp.int32, (3 * LANES, hd), 0)
        sel_c = lax.broadcasted_iota(jnp.int32, (3 * LANES, hd), 1)
        one0 = jnp.where(lax.broadcasted_iota(jnp.int32, (tq, hd), 1) == 0, 1.0, 0.0).astype(BF16)
        for u in range(hps):
            sel = ((sel_r % LANES == h0 + u) & (sel_r // LANES == sel_c)).astype(BF16)
            for r0 in range(0, seq_len, tq):
                rows = slice(r0, r0 + tq)
                ka[u, rows, :hd] = k_ref[rows, cols(u)].astype(BF16)
                ka[u, rows, hd:] = jnp.dot(ck3_ref[rows, :], sel, preferred_element_type=F32).astype(BF16)
                va[u, rows, :hd] = v_ref[rows, cols(u)].astype(BF16)
                va[u, rows, hd:] = one0

    lane = lax.broadcasted_iota(jnp.int32, c_ref.shape, 1)
    lane1 = lax.broadcasted_iota(jnp.int32, (1, LANES), 1)
    ones3 = jnp.where(lax.broadcasted_iota(jnp.int32, (tq, hd), 1) < 3, 1.0, 0.0).astype(BF16)
    cq, qa = [], []
    for u in range(hps):
        cq.append(jnp.broadcast_to(jnp.sum(jnp.where(lane == h0 + u, c_ref[...], 0.0), axis=1, keepdims=True),
                                   (tq, LANES)))
        qa.append(jnp.concatenate([(q_ref[:, cols(u)] * (hd ** -0.5 * LOG2E)).astype(BF16), ones3], axis=1))
    m_s[...] = jnp.full(m_s.shape, -jnp.inf, F32)
    acc[...] = jnp.zeros(acc.shape, F32)
    reps = tk // LANES

    def kv_step(j, masked):
        ks = pl.ds(pl.multiple_of(j * tk, tk), tk)
        rj_all = r_ref[pl.ds(j, 1), :]
        for u in range(hps):
            s = lax.dot_general(qa[u], ka[u, ks, :], NT_DIMS, preferred_element_type=F32)
            if masked:
                key_minus_query = (lax.broadcasted_iota(jnp.int32, (tq, tk), 1)
                                   - lax.broadcasted_iota(jnp.int32, (tq, tk), 0))
                s = jnp.where(key_minus_query <= qi * tq - j * tk, s, -jnp.inf)
            rj = jnp.sum(jnp.where(lane1 == h0 + u, rj_all, 0.0), axis=1, keepdims=True)
            shift = (cq[u] - rj) * LOG2E
            m_old = m_s[u]
            m_new = jnp.maximum(m_old, jnp.broadcast_to(jnp.max(s, axis=1, keepdims=True), (tq, LANES)) + shift)
            p = jnp.exp2(s - jnp.concatenate([m_new - shift] * reps, axis=1))
            a = jnp.exp2(m_old - m_new)
            acc[u] = (jnp.concatenate([a, a], axis=1) * acc[u]
                      + jnp.dot(p.astype(BF16), va[u, ks, :], preferred_element_type=F32))
            m_s[u] = m_new

    n_full = (qi * tq) // tk
    lax.fori_loop(0, n_full, lambda j, c: (kv_step(j, False), c)[1], 0)
    for d in range(tq // tk):
        kv_step(n_full + d, True)
    for u in range(hps):
        o = acc[u]
        o_ref[:, cols(u)] = (o[:, :hd] / jnp.broadcast_to(o[:, hd:hd + 1], (tq, hd))
                             * _silu(g_ref[:, cols(u)])).astype(o_ref.dtype)


def _fox_attend(zq, zk, zv, zg, c, ck3, r, n, seq_len, nh, hd, *, tq, tk, hps):
    assert seq_len % tq == 0 and tq % tk == 0 and r.shape[1] == seq_len // tk and hd == LANES and nh % hps == 0
    r3 = lambda a: a.reshape(n, seq_len, nh * hd)
    qblk = pl.BlockSpec((None, tq, hps * hd), lambda b, h, i: (b, i, h))
    kvblk = pl.BlockSpec((None, seq_len, hps * hd), lambda b, h, i: (b, 0, h))
    return pl.pallas_call(
        partial(_fox_body, tq=tq, tk=tk, hd=hd, hps=hps),
        grid=(n, nh // hps, seq_len // tq),
        in_specs=[qblk, kvblk, kvblk, qblk,
                  pl.BlockSpec((None, tq, LANES), lambda b, h, i: (b, i, 0)),
                  pl.BlockSpec((None, seq_len, 3 * LANES), lambda b, h, i: (b, 0, 0)),
                  pl.BlockSpec((None, seq_len // tk, LANES), lambda b, h, i: (b, 0, 0))],
        out_specs=qblk,
        out_shape=jax.ShapeDtypeStruct((n, seq_len, nh * hd), BF16),
        scratch_shapes=[pltpu.VMEM((hps, seq_len, 2 * hd), BF16), pltpu.VMEM((hps, seq_len, 2 * hd), BF16),
                        pltpu.VMEM((hps, tq, LANES), F32), pltpu.VMEM((hps, tq, 2 * hd), F32)],
        compiler_params=_params("parallel", "parallel", "arbitrary"),
        name="fox_attend",
    )(r3(zq), r3(zk), r3(zv), r3(zg), c, ck3, r).reshape(n * seq_len, nh * hd)


def _page_sums_body(lf_ref, after_ref, tot_ref, *, ps):
    lf = lf_ref[...]
    n = lf.shape[1]
    key = lax.broadcasted_iota(jnp.int32, lf.shape, 1) % ps
    sfx, s = lf, 1
    while s < ps:
        sfx = sfx + jnp.where(key + s < ps, pltpu.roll(sfx, n - s, 1), 0.0)
        s *= 2
    tot, s = jnp.where(key == 0, sfx, 0.0), 1
    while s < ps:
        tot = tot + pltpu.roll(tot, s, 1)
        s *= 2
    src = lax.broadcasted_iota(jnp.int32, (n, n), 0)
    dst = lax.broadcasted_iota(jnp.int32, (n, n), 1)
    perm = (dst == (src % ps) * (n // ps) + src // ps).astype(BF16)
    for x, o_ref in ((sfx - lf, after_ref), (tot, tot_ref)):
        o_ref[...] = sum(jnp.dot(term, perm, preferred_element_type=F32) for term in _split3(x))


def _page_sums(lf_hk, ps, *, rows):
    nl, n_rows, n = lf_hk.shape
    rows = min(rows, n_rows)
    assert n_rows % rows == 0 and n % ps == 0
    blk = pl.BlockSpec((None, rows, n), lambda l, i: (l, i, 0))
    out = jax.ShapeDtypeStruct(lf_hk.shape, F32)
    return pl.pallas_call(
        partial(_page_sums_body, ps=ps),
        grid=(nl, n_rows // rows),
        in_specs=[blk], out_specs=[blk, blk], out_shape=[out, out],
        compiler_params=_params("parallel", "parallel"),
        name="page_forget_sums",
    )(lf_hk)


def _fox_step_body(pt_ref, q_ref, kn_ref, vn_ref, cn_ref, *rest, steps, hd, n_steps, pps):
    kc_refs, vc_refs, after_refs, tot_refs = (rest[k * pps:(k + 1) * pps] for k in range(4))
    o_ref, qb, own_s, m_s, l_s, acc, run = rest[4 * pps:]
    p = pl.program_id(1)
    n_groups = q_ref.shape[0]
    rows = steps * SUBLANES
    keys = kc_refs[0].shape[0] * SUBLANES

    def same_head(n_keys):
        r = lax.broadcasted_iota(jnp.int32, (rows, n_keys), 0) % SUBLANES
        c = lax.broadcasted_iota(jnp.int32, (rows, n_keys), 1) % SUBLANES
        return jnp.where(r == c, 0.0, -jnp.inf)

    def attend(k_b, v_b, bias):
        groups = range(n_groups)
        s = [lax.dot_general(qb[g], k_b[g], NT_DIMS, preferred_element_type=F32) + bias[g] for g in groups]
        m_old = [m_s[g] for g in groups]
        m_new = [jnp.maximum(m_old[g], jnp.max(s[g], axis=1, keepdims=True)) for g in groups]
        e = [jnp.exp2(s[g] - m_new[g]) for g in groups]
        a = [jnp.exp2(m_old[g] - m_new[g]) for g in groups]
        pv = [jnp.dot(e[g].astype(BF16), v_b[g], preferred_element_type=F32) for g in groups]
        for g in groups:
            l_s[g] = a[g] * l_s[g] + jnp.sum(e[g], axis=1, keepdims=True)
            acc[g] = a[g] * acc[g] + pv[g]
            m_s[g] = m_new[g]

    @pl.when(p == 0)
    def _():
        m_s[...] = jnp.full(m_s.shape, -jnp.inf, F32)
        l_s[...] = jnp.zeros(l_s.shape, F32)
        acc[...] = jnp.zeros(acc.shape, F32)
        run[...] = jnp.zeros(run.shape, F32)
        own_s[...] = same_head(keys)
        key_step = lax.broadcasted_iota(jnp.int32, (rows, rows), 1) // SUBLANES
        q_step = lax.broadcasted_iota(jnp.int32, (rows, rows), 0) // SUBLANES
        visible = jnp.where(key_step <= q_step, same_head(rows), -jnp.inf)
        for g in range(n_groups):
            qb[g] = (q_ref[g] * (hd ** -0.5 * LOG2E)).astype(BF16)
        attend([kn_ref[g].astype(BF16) for g in range(n_groups)],
               [vn_ref[g].astype(BF16) for g in range(n_groups)],
               [visible - LOG2E * cn_ref[g] for g in range(n_groups)])

    running, bias = run[...], []
    for k in range(pps):
        bias.append(LOG2E * (after_refs[k][...] + running))
        running = running + tot_refs[k][...]
    run[...] = running
    page_rows = lambda refs, g: jnp.concatenate([r[:, g].reshape(keys, hd).astype(BF16) for r in refs], axis=0)
    attend([page_rows(kc_refs, g) for g in range(n_groups)],
           [page_rows(vc_refs, g) for g in range(n_groups)],
           [jnp.concatenate([own_s[...] + b[g:g + 1, :] for b in bias], axis=1) for g in range(n_groups)])

    @pl.when(p == n_steps - 1)
    def _():
        o_ref[...] = acc[...] / l_s[...]


def _fox_attend_step(q, k_new, v_new, c_new, kc, vc, after, tot, page_table, layer, steps, *, pps):
    n, n_groups, rows, hd = q.shape
    ps = kc.shape[2]
    keys = ps * SUBLANES
    n_pages = page_table.shape[1]
    assert rows == steps * SUBLANES and kc.shape[3:] == (n_groups, SUBLANES, hd) and keys % LANES == 0
    assert n_pages % pps == 0
    per_seq = lambda shape: pl.BlockSpec((None,) + shape, lambda b, p, pt: (b,) + (0,) * len(shape))

    def pages(shape):
        return [pl.BlockSpec((None, None) + shape,
                             lambda b, p, pt, k=k: (layer, pt[(b + 1) * n_pages - 1 - p * pps - k]) + (0,) * len(shape))
                for k in range(pps)]

    grid_spec = pltpu.PrefetchScalarGridSpec(
        num_scalar_prefetch=1,
        grid=(n, n_pages // pps),
        in_specs=[per_seq((n_groups, rows, hd)), per_seq((n_groups, rows, hd)), per_seq((n_groups, rows, hd)),
                  per_seq((n_groups, 1, rows)),
                  *pages((ps, n_groups, SUBLANES, hd)), *pages((ps, n_groups, SUBLANES, hd)),
                  *pages((n_groups, keys)), *pages((n_groups, keys))],
        out_specs=per_seq((n_groups, rows, hd)),
        scratch_shapes=[pltpu.VMEM((n_groups, rows, hd), BF16), pltpu.VMEM((rows, keys), F32),
                        pltpu.VMEM((n_groups, rows, 1), F32), pltpu.VMEM((n_groups, rows, 1), F32),
                        pltpu.VMEM((n_groups, rows, hd), F32), pltpu.VMEM((n_groups, keys), F32)],
    )
    return pl.pallas_call(
        partial(_fox_step_body, steps=steps, hd=hd, n_steps=n_pages // pps, pps=pps),
        grid_spec=grid_spec,
        out_shape=jax.ShapeDtypeStruct((n, n_groups, rows, hd), F32),
        compiler_params=_params("arbitrary", "arbitrary"),
        name="fox_attend_step",
    )(page_table.reshape(-1), q, k_new, v_new, c_new, *([kc] * pps), *([vc] * pps), *([after] * pps), *([tot] * pps))


def _offsets(sizes):
    out, o = [], 0
    for s in sizes:
        out.append(o)
        o += s
    return out, o


def _ssm_conv_mix(z, seq_len, state_re, state_im, conv_state, mem_k, mem_v, s5p, glu_w, glu_b, conv_w, conv_b,
                  i, layer, nh_mem, cols, *, decode):
    m = z.shape[0]
    n = m // seq_len
    w, c = glu_w.shape[1], conv_w.shape[2]
    o_ua, o_ga, o_bg, o_cg, o_hb, o_gb, o_qm, o_gm = cols
    sps = seq_len if decode else seq_len // SUBLANES
    if decode:
        to_scan = lambda a: a.reshape(n, seq_len, -1).transpose(1, 0, 2)
        u = to_scan(z[:, o_ua:o_ua + w]).reshape(1, m, w)
        h0_re, h0_im = state_re.reshape(1, n, -1), state_im.reshape(1, n, -1)
    else:
        u = z[:, o_ua:o_ua + w].reshape(n, SUBLANES, sps, w).transpose(0, 2, 1, 3).reshape(n, seq_len, w)
        pad = lambda s: jnp.pad(s.reshape(n, 1, -1), ((0, 0), (0, SUBLANES - 1), (0, 0)))
        h0_re, h0_im = pad(state_re), pad(state_im)
    y, h_re, h_im = _s5(u, h0_re, h0_im, s5p, chain=not decode)
    if decode:
        y = y.reshape(seq_len, n, w).transpose(1, 0, 2).reshape(m, w)
        h_re, h_im = h_re[0], h_im[0]
    else:
        y = y.reshape(n, sps, SUBLANES, w).transpose(0, 2, 1, 3).reshape(m, w)
        h_re, h_im = h_re[:, SUBLANES - 1], h_im[:, SUBLANES - 1]
    mix_dtype = F32 if decode else BF16
    mix_a = _glu_gate(y, z, o_ga, glu_w, glu_b, i, tm=1024, tn=512, out_dtype=mix_dtype)
    if decode:
        bg, cg, hb, gb = (to_scan(z[:, o:o + c]) for o in (o_bg, o_cg, o_hb, o_gb))
        mix_b, conv_new = _conv_gate_step(bg, cg, hb, gb, conv_state.transpose(1, 0, 2), conv_w, conv_b, i, ct=512)
        mix_b = mix_b.transpose(1, 0, 2).reshape(m, c)
        conv_new = conv_new.transpose(1, 0, 2)
    else:
        mix_b, conv_new = _conv_gate(z, (o_bg, o_cg, o_hb, o_gb), c, conv_state, conv_w, conv_b, i, seq_len,
                                     tm=512, ct=512)
    mix_m = _mem_attend(z, o_qm, o_gm, mem_k, mem_v, layer, seq_len, nh_mem, tq=512, joint=decode,
                        out_dtype=mix_dtype)
    gp = state_re.shape[1:]
    return (mix_a, mix_b, mix_m), (h_re.reshape((n,) + gp), h_im.reshape((n,) + gp), conv_new)


def _ssm_conv_layer(xp, xs, t_p, t_s, states_s, mem_p, mem_s, w_in, w_out, s5p, glu_w, glu_b, conv_w, conv_b,
                    ln_g, ln_b, i, layer, alpha, nh_mem):
    d = xp[0].shape[1]
    n_p = xp[0].shape[0] // t_p
    w, c, mw = glu_w.shape[1], conv_w.shape[2], mem_p[0].shape[2]
    cols, n_cols = _offsets((w, w, c, c, c, c, mw, mw))
    z_p, z_s = _matmul([(xp[1], xs[1], 0, w_in, i, 0, 0, d)], n_cols, tm=1024, tn=512, name="in_proj_ssm")
    zero_state = jnp.zeros((n_p,) + states_s[0].shape[1:], F32)
    zero_conv = jnp.zeros((n_p,) + states_s[2].shape[1:], F32)
    branch = (s5p, glu_w, glu_b, conv_w, conv_b, i, layer, nh_mem, cols)
    mix_p, st_p = _ssm_conv_mix(z_p, t_p, zero_state, zero_state, zero_conv, *mem_p, *branch, decode=False)
    mix_s, st_s = _ssm_conv_mix(z_s, t_s, *states_s, *mem_s, *branch, decode=True)
    out_proj = lambda mix, x: _matmul([(mix[0], None, 0, w_out, i, 0, 0, w), (mix[1], None, 0, w_out, i, w // c, 0, c),
                                       (mix[2], None, 0, w_out, i, (w + c) // mw, 0, mw)],
                                      d, tm=1024, tn=256, res=x[0], alpha=alpha, name="out_proj_ssm")
    y_p, y_s = out_proj(mix_p, xp), out_proj(mix_s, xs)
    return _layer_norm(y_p, ln_g, ln_b, layer, tm=256), _layer_norm(y_s, ln_g, ln_b, layer, tm=256), st_p, st_s


def _fox_layer(xp, xs, t_p, t_s, past, mem_p, mem_s, w_in_t, w_tail_t, b_f128, w_out, ln_g, ln_b, i, layer, alpha,
               nh, nh_mem):
    d = xp[0].shape[1]
    m_p, m_s = xp[0].shape[0], xs[0].shape[0]
    n_p, n_s = m_p // t_p, m_s // t_s
    mw = mem_p[0].shape[2]
    fw = w_out.shape[1] - mw
    hd = fw // nh
    proj = [_matmul([(xp[1], xs[1], 0, w_in_t, i, 0, o * fw, d)], fw, tm=1024, tn=512, transposed=True,
                    name="in_proj_fox") for o in range(4)]
    (zq_p, zq_s), (zk_p, zk_s), (zv_p, zv_s), (zg_p, zg_s) = proj
    tail_p, tail_s = _matmul([(xp[1], xs[1], 0, w_tail_t, i, 0, 0, d)], w_tail_t.shape[1], tm=1024, tn=384,
                             transposed=True, name="in_proj_fox_tail")
    o_qm, o_gm, o_f = 0, mw, 2 * mw

    tq = tk = min(512, t_p)
    lf_p, c, ck3, r = _forget_gates(tail_p, o_f, b_f128, i, n_p, t_p, tk=tk)
    mix_f_p = _fox_attend(zq_p, zk_p, zv_p, zg_p, c, ck3, r, n_p, t_p, nh, hd, tq=tq, tk=tk, hps=4)
    mix_m_p = _mem_attend(tail_p, o_qm, o_gm, *mem_p, layer, t_p, nh_mem, tq=512, joint=False, out_dtype=BF16)

    kc, vc, after, tot, page_table = past
    ng = nh // SUBLANES
    lf_s, c_new = _forget_gates_step(tail_s, o_f, b_f128, i, t_s)
    grouped = lambda a: (a.reshape(n_s, t_s, ng, SUBLANES, -1).transpose(0, 2, 1, 3, 4)
                         .reshape(n_s, ng, t_s * SUBLANES, -1))
    c_new = grouped(c_new[:, :nh]).reshape(n_s, ng, 1, t_s * SUBLANES)
    y = _fox_attend_step(grouped(zq_s), grouped(zk_s), grouped(zv_s), c_new, kc, vc, after, tot, page_table, i, t_s,
                         pps=2)
    y = y.reshape(n_s, ng, t_s, SUBLANES, hd).transpose(0, 2, 1, 3, 4).reshape(m_s, fw)
    mix_f_s = _gate_step(y, zg_s)
    mix_m_s = _mem_attend(tail_s, o_qm, o_gm, *mem_s, layer, t_s, nh_mem, tq=512, joint=True, out_dtype=F32)

    out_proj = lambda mix_f, mix_m, x: _matmul([(mix_f, None, 0, w_out, i, 0, 0, fw),
                                                (mix_m, None, 0, w_out, i, fw // mw, 0, mw)],
                                               d, tm=1024, tn=256, res=x[0], alpha=alpha, name="out_proj_fox")
    y_p, y_s = out_proj(mix_f_p, mix_m_p, xp), out_proj(mix_f_s, mix_m_s, xs)
    kv_p = (zk_p.reshape(n_p, t_p, nh, hd), zv_p.reshape(n_p, t_p, nh, hd), lf_p[:, :, :nh])
    kv_s = (zk_s.reshape(n_s, t_s, nh, hd), zv_s.reshape(n_s, t_s, nh, hd), lf_s[:, :nh].reshape(n_s, t_s, nh))
    return _layer_norm(y_p, ln_g, ln_b, layer, tm=256), _layer_norm(y_s, ln_g, ln_b, layer, tm=256), kv_p, kv_s


def _gate_step_body(y_ref, g_ref, o_ref):
    o_ref[...] = y_ref[...] * _silu(g_ref[...])


def _gate_step(y, g):
    m, w = y.shape
    blk = pl.BlockSpec((m, w), lambda i: (0, 0))
    return pl.pallas_call(_gate_step_body, grid=(1,), in_specs=[blk, blk], out_specs=blk,
                          out_shape=jax.ShapeDtypeStruct((m, w), F32), compiler_params=_params("arbitrary"),
                          name="gate_step")(y, g)


def kernel(x_prompt, x_sample, cache_fox_k, cache_fox_v, cache_fox_lf, cache_mem_k, cache_mem_v, state_ssm_re, state_ssm_im, state_conv, page_table, mem_prompt, w_in_ssm, w_out_ssm, ssm_lam_re, ssm_lam_im, ssm_log_dt, ssm_b_re, ssm_b_im, ssm_c_re, ssm_c_im, ssm_d, glu_w, glu_b, conv_w, conv_b, w_in_fox, b_fgate, w_out_fox, w_mem_k, w_mem_v, ln_g, ln_b):
    n_p, t_p, d = x_prompt.shape
    n_s, t_s, _ = x_sample.shape
    depth = ln_g.shape[0]
    alpha = (2 * depth) ** 0.25
    nh_fox = b_fgate.shape[1]
    nh_mem = cache_mem_k.shape[3]
    mw = w_mem_k.shape[2]
    mem_len = mem_prompt.shape[1]
    fw = w_out_fox.shape[1] - mw
    n_fox, n_pool, ps = cache_fox_k.shape[:3]

    w_in_t = w_in_fox.transpose(0, 2, 1)
    w_tail_t = jnp.concatenate([w_in_t[:, 4 * fw + nh_fox:], w_in_t[:, 4 * fw:4 * fw + nh_fox]], axis=1)
    w_tail_t = jnp.pad(w_tail_t, ((0, 0), (0, LANES - nh_fox), (0, 0)))
    b_f128 = jnp.pad(b_fgate.astype(F32), ((0, 0), (0, LANES - nh_fox))).reshape(n_fox, 1, LANES)
    assert nh_fox % SUBLANES == 0
    ng = nh_fox // SUBLANES
    kc = cache_fox_k.reshape(n_fox, n_pool, ps, ng, SUBLANES, fw // nh_fox)
    vc = cache_fox_v.reshape(n_fox, n_pool, ps, ng, SUBLANES, fw // nh_fox)
    lf_hk = (cache_fox_lf.astype(F32).transpose(0, 1, 3, 2)
             .reshape(n_fox, n_pool * ng, SUBLANES * ps))
    lf_after, lf_tot = (a.reshape(n_fox, n_pool, ng, ps * SUBLANES) for a in _page_sums(lf_hk, ps, rows=512))
    past = (kc, vc, lf_after, lf_tot, page_table)
    mem_s = (cache_mem_k.reshape(depth, n_s * mem_len, mw), cache_mem_v.reshape(depth, n_s * mem_len, mw))

    mem_in = mem_prompt.reshape(n_p * mem_len, d).astype(BF16)
    mem_p = (_memproj(mem_in, w_mem_k), _memproj(mem_in, w_mem_v))

    xp = (x_prompt.reshape(n_p * t_p, d),)
    xs = (x_sample.reshape(n_s * t_s, d),)
    xp, xs = xp + (xp[0].astype(BF16),), xs + (xs[0].astype(BF16),)

    fox_p, fox_s, ssm_p, ssm_s = [], [], [], []
    for layer in range(depth):
        i = layer // 2
        if layer % 2 == 0:
            s5p = _s5_discretise(ssm_lam_re[i], ssm_lam_im[i], ssm_log_dt[i], ssm_b_re[i], ssm_b_im[i],
                                 ssm_c_re[i], ssm_c_im[i], ssm_d[i])
            states_s = (state_ssm_re[i].astype(F32), state_ssm_im[i].astype(F32), state_conv[i].astype(F32))
            xp, xs, st_p, st_s = _ssm_conv_layer(xp, xs, t_p, t_s, states_s, mem_p, mem_s, w_in_ssm, w_out_ssm, s5p,
                                                 glu_w, glu_b, conv_w, conv_b, ln_g, ln_b, i, layer, alpha, nh_mem)
            ssm_p.append(st_p), ssm_s.append(st_s)
        else:
            xp, xs, kv_p, kv_s = _fox_layer(xp, xs, t_p, t_s, past, mem_p, mem_s, w_in_t, w_tail_t, b_f128,
                                            w_out_fox, ln_g, ln_b, i, layer, alpha, nh_fox, nh_mem)
            fox_p.append(kv_p), fox_s.append(kv_s)
    stack = lambda rows, k: jnp.stack([r[k] for r in rows])
    mem_shape = (depth, n_p, mem_len, nh_mem, mw // nh_mem)
    return (xp[0].reshape(n_p, t_p, d), xs[0].reshape(n_s, t_s, d),
            stack(fox_p, 0), stack(fox_p, 1), stack(fox_p, 2),
            stack(fox_s, 0), stack(fox_s, 1), stack(fox_s, 2),
            stack(ssm_p, 0), stack(ssm_p, 1), stack(ssm_p, 2),
            stack(ssm_s, 0), stack(ssm_s, 1), stack(ssm_s, 2),
            mem_p[0].reshape(mem_shape), mem_p[1].reshape(mem_shape))
```

```python
from functools import partial

import jax
import jax.numpy as jnp
from jax import lax
from jax.experimental import pallas as pl
from jax.experimental.pallas import tpu as pltpu

F32 = jnp.float32
BF16 = jnp.bfloat16

LN_EPS = 1e-5
LANES = 128
SUBLANES = 8
VMEM_LIMIT_BYTES = 56 * 1024 * 1024
NT_DIMS = (((1,), (1,)), ((), ()))
LOG2E = 1.4426950408889634


def _params(*sem):
    return pltpu.CompilerParams(dimension_semantics=sem, vmem_limit_bytes=VMEM_LIMIT_BYTES)


def _silu(g):
    return g * jax.nn.sigmoid(g)


def _mm_body(*refs, n_pairs, transposed, alpha, has_res, has_side):
    per = 3 if has_side else 2
    n_in = per * n_pairs + (per - 1 if has_res else 0)
    out_ref = refs[n_in]
    wb = [refs[per * p + 1][...].astype(BF16) for p in range(n_pairs)]

    def product(x_of_pair, res_ref):
        acc = None
        for p in range(n_pairs):
            x = x_of_pair(p)[...].astype(BF16)
            if transposed:
                d = lax.dot_general(x, wb[p], NT_DIMS, preferred_element_type=F32)
            else:
                d = jnp.dot(x, wb[p], preferred_element_type=F32)
            acc = d if acc is None else acc + d
        return acc if res_ref is None else alpha * res_ref[...] + acc

    out_ref[...] = product(lambda p: refs[per * p], refs[per * n_pairs] if has_res else None).astype(out_ref.dtype)
    if has_side:
        side_ref = refs[n_in + 1]

        @pl.when(pl.program_id(0) == 0)
        def _():
            side_ref[...] = product(lambda p: refs[3 * p + 2], refs[3 * n_pairs + 1] if has_res else None)

        @pl.when(pl.program_id(0) != 0)
        def _():
            side_ref[...] = jnp.zeros(side_ref.shape, F32)


def _matmul(pairs, n_out, *, tm, tn, res=None, alpha=1.0, out_dtype=F32, transposed=False, name="mm"):
    has_side = pairs[0][1] is not None
    m = pairs[0][0].shape[0]
    tm, tn = min(tm, m), min(tn, n_out)
    assert m % tm == 0 and n_out % tn == 0
    in_specs, args = [], []
    for (x, xs, xkb, w, layer, wkb, col0, kdim) in pairs:
        assert col0 % tn == 0
        cb = col0 // tn
        in_specs.append(pl.BlockSpec((tm, kdim), lambda i, j, xkb=xkb: (i, xkb)))
        if transposed:
            in_specs.append(pl.BlockSpec((None, tn, kdim), lambda i, j, l=layer, wkb=wkb, cb=cb: (l, cb + j, wkb)))
        else:
            in_specs.append(pl.BlockSpec((None, kdim, tn), lambda i, j, l=layer, wkb=wkb, cb=cb: (l, wkb, cb + j)))
        args += [x, w]
        if has_side:
            in_specs.append(pl.BlockSpec((xs.shape[0], kdim), lambda i, j, xkb=xkb: (0, xkb)))
            args.append(xs)
    out_specs = [pl.BlockSpec((tm, tn), lambda i, j: (i, j))]
    out_shape = [jax.ShapeDtypeStruct((m, n_out), out_dtype)]
    if res is not None:
        in_specs.append(pl.BlockSpec((tm, tn), lambda i, j: (i, j)))
        args.append(res[0] if has_side else res)
    if has_side:
        ms = pairs[0][1].shape[0]
        if res is not None:
            in_specs.append(pl.BlockSpec((ms, tn), lambda i, j: (0, j)))
            args.append(res[1])
        out_specs.append(pl.BlockSpec((None, ms, tn), lambda i, j: (i, 0, j)))
        out_shape.append(jax.ShapeDtypeStruct((m // tm, ms, n_out), F32))
    out = pl.pallas_call(
        partial(_mm_body, n_pairs=len(pairs), transposed=transposed, alpha=alpha, has_res=res is not None,
                has_side=has_side),
        grid=(m // tm, n_out // tn),
        in_specs=in_specs,
        out_specs=out_specs,
        out_shape=out_shape,
        compiler_params=_params("parallel", "arbitrary"),
        name=name,
    )(*args)
    return (out[0], out[1][0]) if has_side else out[0]


def _memproj_body(x_ref, w_ref, o_ref):
    o_ref[...] = jnp.dot(x_ref[...].astype(BF16), w_ref[...].astype(BF16), preferred_element_type=F32)


def _memproj(x, w):
    nl, d, n = w.shape
    r = x.shape[0]
    return pl.pallas_call(
        _memproj_body,
        grid=(nl,),
        in_specs=[pl.BlockSpec((r, d), lambda l: (0, 0)), pl.BlockSpec((None, d, n), lambda l: (l, 0, 0))],
        out_specs=pl.BlockSpec((None, r, n), lambda l: (l, 0, 0)),
        out_shape=jax.ShapeDtypeStruct((nl, r, n), F32),
        compiler_params=_params("arbitrary"),
        name="memproj",
    )(x, w)


def _ln_body(y_ref, g_ref, b_ref, xf_ref, xb_ref):
    y = y_ref[...]
    mu = jnp.mean(y, axis=-1, keepdims=True)
    d = y - mu
    var = jnp.mean(d * d, axis=-1, keepdims=True)
    out = d * lax.rsqrt(var + LN_EPS) * g_ref[...] + b_ref[...]
    xf_ref[...] = out
    xb_ref[...] = out.astype(BF16)


def _layer_norm(y, g, b, layer, *, tm):
    m, d = y.shape
    tm = min(tm, m)
    g3 = g.reshape(g.shape[0], 1, d)
    b3 = b.reshape(b.shape[0], 1, d)
    vec = pl.BlockSpec((None, 1, d), lambda i: (layer, 0, 0))
    row = pl.BlockSpec((tm, d), lambda i: (i, 0))
    return pl.pallas_call(
        _ln_body,
        grid=(m // tm,),
        in_specs=[row, vec, vec],
        out_specs=[row, row],
        out_shape=[jax.ShapeDtypeStruct((m, d), F32), jax.ShapeDtypeStruct((m, d), BF16)],
        compiler_params=_params("parallel"),
        name="layernorm",
    )(y, g3, b3)


def _s5_discretise(lam_re, lam_im, log_dt, b_re, b_im, c_re, c_im, d_skip):
    g, p, h = b_re.shape
    gpc = LANES // h
    ncc = g // gpc
    lam_re, lam_im = lam_re.astype(F32), lam_im.astype(F32)
    dt = jnp.exp(log_dt.astype(F32))[:, None]
    mag = jnp.exp(lam_re * dt)
    ar = mag * jnp.cos(lam_im * dt)
    ai = mag * jnp.sin(lam_im * dt)
    nr = ar - 1.0
    den = jnp.square(lam_re) + jnp.square(lam_im)
    cr = ((nr * lam_re + ai * lam_im) / den)[..., None]
    ci = ((ai * lam_re - nr * lam_im) / den)[..., None]
    b_re, b_im = b_re.astype(F32), b_im.astype(F32)
    bb_re = cr * b_re - ci * b_im
    bb_im = cr * b_im + ci * b_re
    eye = jnp.eye(gpc, dtype=F32)

    def pack_in(bb):
        return jnp.einsum('cgph,gk->cghkp', bb.reshape(ncc, gpc, p, h), eye).reshape(ncc, gpc * h, gpc * p)

    def pack_out(c):
        return jnp.einsum('cghp,gk->cgpkh', c.astype(F32).reshape(ncc, gpc, h, p), eye).reshape(ncc, gpc * p, gpc * h)

    return dict(a_re=ar.reshape(ncc, 1, gpc * p), a_im=ai.reshape(ncc, 1, gpc * p),
                wb_re=pack_in(bb_re), wb_im=pack_in(bb_im), wc_re=pack_out(c_re), wc_im=pack_out(c_im),
                d=d_skip.astype(F32).reshape(ncc, 1, gpc * h))


def _s5_body(u_ref, h0r_ref, h0i_ref, ar_ref, ai_ref, wbr_ref, wbi_ref, wcr_ref, wci_ref, d_ref,
             y_ref, hr_out, hi_out, hre, him, *, steps, chain):
    u = u_ref[...]
    ub = u.astype(BF16)
    hre[...] = jnp.dot(ub, wbr_ref[...].astype(BF16), preferred_element_type=F32)
    him[...] = jnp.dot(ub, wbi_ref[...].astype(BF16), preferred_element_type=F32)
    ns = hre.shape[1]
    ar = jnp.broadcast_to(ar_ref[...], (SUBLANES, ns))
    ai = jnp.broadcast_to(ai_ref[...], (SUBLANES, ns))

    def rows_of(t):
        return pl.ds(pl.multiple_of(t * SUBLANES, SUBLANES), SUBLANES)

    def scan_step(t, c):
        hr, hi = c
        rows = rows_of(t)
        nr = ar * hr - ai * hi + hre[rows, :]
        ni = ar * hi + ai * hr + him[rows, :]
        hre[rows, :] = nr
        him[rows, :] = ni
        return nr, ni

    fr, fi = lax.fori_loop(0, steps, scan_step, (h0r_ref[...], h0i_ref[...]), unroll=2)

    if chain:
        alr, ali = None, None
        sqr, sqi, e = ar, ai, steps
        while e:
            if e & 1:
                alr, ali = (sqr, sqi) if alr is None else (alr * sqr - ali * sqi, alr * sqi + ali * sqr)
            sqr, sqi = sqr * sqr - sqi * sqi, 2.0 * sqr * sqi
            e >>= 1
        sub = lax.broadcasted_iota(jnp.int32, (SUBLANES, ns), 0)
        sr, si = fr, fi
        for c in range(1, SUBLANES):
            pr, pi = pltpu.roll(sr, 1, 0), pltpu.roll(si, 1, 0)
            sr = jnp.where(sub == c, alr * pr - ali * pi + fr, sr)
            si = jnp.where(sub == c, alr * pi + ali * pr + fi, si)
        cin_r = jnp.where(sub == 0, 0.0, pltpu.roll(sr, 1, 0))
        cin_i = jnp.where(sub == 0, 0.0, pltpu.roll(si, 1, 0))

        def fix_step(t, c):
            gr, gi = c
            rows = rows_of(t)
            nr, ni = ar * gr - ai * gi, ar * gi + ai * gr
            hre[rows, :] = hre[rows, :] + nr
            him[rows, :] = him[rows, :] + ni
            return nr, ni

        lax.fori_loop(0, steps, fix_step, (cin_r, cin_i), unroll=2)
        fr, fi = sr, si

    hr_out[...] = fr
    hi_out[...] = fi
    y = (jnp.dot(hre[...].astype(BF16), wcr_ref[...].astype(BF16), preferred_element_type=F32)
         - jnp.dot(him[...].astype(BF16), wci_ref[...].astype(BF16), preferred_element_type=F32)
         + d_ref[...] * u)
    y_ref[...] = y


def _s5(u, h0_re, h0_im, prm, *, chain):
    nb, rows, w = u.shape
    steps = rows // SUBLANES
    ncc, cw, ns = prm["wb_re"].shape
    assert w == ncc * cw and cw == LANES
    seq = pl.BlockSpec((None, rows, cw), lambda b, c: (b, 0, c))
    st = pl.BlockSpec((None, SUBLANES, ns), lambda b, c: (b, 0, c))

    def par(shape):
        return pl.BlockSpec((None,) + shape, lambda b, c: (c, 0, 0))

    return pl.pallas_call(
        partial(_s5_body, steps=steps, chain=chain),
        grid=(nb, ncc),
        in_specs=[seq, st, st, par((1, ns)), par((1, ns)), par((cw, ns)), par((cw, ns)),
                  par((ns, cw)), par((ns, cw)), par((1, cw))],
        out_specs=[seq, st, st],
        out_shape=[jax.ShapeDtypeStruct((nb, rows, w), F32),
                   jax.ShapeDtypeStruct((nb, SUBLANES, ncc * ns), F32),
                   jax.ShapeDtypeStruct((nb, SUBLANES, ncc * ns), F32)],
        scratch_shapes=[pltpu.VMEM((rows, ns), F32), pltpu.VMEM((rows, ns), F32)],
        compiler_params=_params("parallel", "arbitrary"),
        name="s5_scan",
    )(u, h0_re, h0_im, prm["a_re"], prm["a_im"], prm["wb_re"], prm["wb_im"], prm["wc_re"], prm["wc_im"], prm["d"])


def _glu_body(yfull_ref, yblk_ref, ga_ref, w_ref, b_ref, o_ref, zb):
    @pl.when(pl.program_id(1) == 0)
    def _():
        zb[...] = jax.nn.gelu(yfull_ref[...]).astype(BF16)

    glu = jnp.dot(zb[...], w_ref[...].astype(BF16), preferred_element_type=F32) + b_ref[...]
    z = jax.nn.gelu(yblk_ref[...])
    o_ref[...] = (z * jax.nn.sigmoid(glu) * _silu(ga_ref[...])).astype(o_ref.dtype)


def _glu_gate(y, z_all, ga_col0, glu_w, glu_b, layer, *, tm, tn, out_dtype):
    m, w = y.shape
    tm, tn = min(tm, m), min(tn, w)
    assert ga_col0 % tn == 0
    gb3 = glu_b.reshape(glu_b.shape[0], 1, w)
    return pl.pallas_call(
        _glu_body,
        grid=(m // tm, w // tn),
        in_specs=[pl.BlockSpec((tm, w), lambda i, j: (i, 0)),
                  pl.BlockSpec((tm, tn), lambda i, j: (i, j)),
                  pl.BlockSpec((tm, tn), lambda i, j: (i, ga_col0 // tn + j)),
                  pl.BlockSpec((None, w, tn), lambda i, j: (layer, 0, j)),
                  pl.BlockSpec((None, 1, tn), lambda i, j: (layer, 0, j))],
        out_specs=pl.BlockSpec((tm, tn), lambda i, j: (i, j)),
        out_shape=jax.ShapeDtypeStruct((m, w), out_dtype),
        scratch_shapes=[pltpu.VMEM((tm, w), BF16)],
        compiler_params=_params("parallel", "arbitrary"),
        name="glu_gate",
    )(y, y, z_all, glu_w, gb3)


def _conv_body(bg_ref, cg_ref, hb_ref, gb_ref, cs_ref, w_ref, b_ref, o_ref, cn_ref, prev, *, tiles_per_seq):
    @pl.when(pl.program_id(1) % tiles_per_seq == 0)
    def _():
        prev[...] = cs_ref[...]

    v = cg_ref[...] * hb_ref[...]
    tm = v.shape[0]
    p2, p1 = prev[0:1, :], prev[1:2, :]
    row = lax.broadcasted_iota(jnp.int32, v.shape, 0)
    v1 = jnp.where(row == 0, p1, pltpu.roll(v, 1, 0))
    v2 = jnp.where(row == 0, p2, jnp.where(row == 1, p1, pltpu.roll(v, 2, 0)))
    y = b_ref[...] + w_ref[0:1, :] * v2 + w_ref[1:2, :] * v1 + w_ref[2:3, :] * v
    o_ref[...] = (bg_ref[...] * y * _silu(gb_ref[...])).astype(o_ref.dtype)
    last = v[tm - 2:tm, :]
    prev[...] = last
    cn_ref[...] = last


def _conv_gate(z_all, col0, c, conv_state, conv_w, conv_b, layer, seq_len, *, tm, ct):
    m = z_all.shape[0]
    tm, ct = min(tm, seq_len), min(ct, c)
    assert seq_len % tm == 0 and all(o % ct == 0 for o in col0) and conv_w.shape[1] == 3
    tps = seq_len // tm
    n = m // seq_len
    cb3 = conv_b.reshape(conv_b.shape[0], 1, c)

    def zcol(o):
        return pl.BlockSpec((tm, ct), lambda j, i, o=o: (i, o // ct + j))

    st = pl.BlockSpec((None, 2, ct), lambda j, i: (i // tps, 0, j))
    return pl.pallas_call(
        partial(_conv_body, tiles_per_seq=tps),
        grid=(c // ct, m // tm),
        in_specs=[zcol(col0[0]), zcol(col0[1]), zcol(col0[2]), zcol(col0[3]), st,
                  pl.BlockSpec((None, 3, ct), lambda j, i: (layer, 0, j)),
                  pl.BlockSpec((None, 1, ct), lambda j, i: (layer, 0, j))],
        out_specs=[pl.BlockSpec((tm, ct), lambda j, i: (i, j)), st],
        out_shape=[jax.ShapeDtypeStruct((m, c), BF16), jax.ShapeDtypeStruct((n, 2, c), F32)],
        scratch_shapes=[pltpu.VMEM((2, ct), F32)],
        compiler_params=_params("parallel", "arbitrary"),
        name="conv_gate",
    )(z_all, z_all, z_all, z_all, conv_state, conv_w, cb3)


def _conv_step_body(bg_ref, cg_ref, hb_ref, gb_ref, cs_ref, w_ref, b_ref, o_ref, cn_ref, *, steps):
    vp = [cs_ref[0], cs_ref[1]] + [cg_ref[t] * hb_ref[t] for t in range(steps)]
    for t in range(steps):
        y = b_ref[...] + w_ref[0:1, :] * vp[t] + w_ref[1:2, :] * vp[t + 1] + w_ref[2:3, :] * vp[t + 2]
        o_ref[t] = bg_ref[t] * y * _silu(gb_ref[t])
    cn_ref[0] = vp[steps]
    cn_ref[1] = vp[steps + 1]


def _conv_gate_step(bg, cg, hb, gb, conv_state_t, conv_w, conv_b, layer, *, ct):
    steps, n, c = bg.shape
    ct = min(ct, c)
    assert conv_w.shape[1] == 3
    cb3 = conv_b.reshape(conv_b.shape[0], 1, c)
    blk = pl.BlockSpec((steps, n, ct), lambda j: (0, 0, j))
    st = pl.BlockSpec((2, n, ct), lambda j: (0, 0, j))
    return pl.pallas_call(
        partial(_conv_step_body, steps=steps),
        grid=(c // ct,),
        in_specs=[blk, blk, blk, blk, st,
                  pl.BlockSpec((None, 3, ct), lambda j: (layer, 0, j)),
                  pl.BlockSpec((None, 1, ct), lambda j: (layer, 0, j))],
        out_specs=[blk, st],
        out_shape=[jax.ShapeDtypeStruct((steps, n, c), F32), jax.ShapeDtypeStruct((2, n, c), F32)],
        compiler_params=_params("parallel"),
        name="conv_gate_step",
    )(bg, cg, hb, gb, conv_state_t, conv_w, cb3)


def _mem_body(q_ref, g_ref, k_ref, v_ref, o_ref, *, nh, hd, rows_per_seq, keys_per_seq):
    scale = hd ** -0.5
    tq, nk = q_ref.shape[0], k_ref.shape[0]
    if rows_per_seq is not None:
        rseq = lax.broadcasted_iota(jnp.int32, (tq, nk), 0) // rows_per_seq
        kseq = lax.broadcasted_iota(jnp.int32, (tq, nk), 1) // keys_per_seq
        own = rseq == kseq
    for h in range(nh):
        sl = slice(h * hd, (h + 1) * hd)
        q = q_ref[:, sl].astype(BF16)
        k = k_ref[:, sl].astype(BF16)
        v = v_ref[:, sl].astype(BF16)
        s = lax.dot_general(q, k, NT_DIMS, preferred_element_type=F32) * scale
        if rows_per_seq is not None:
            s = jnp.where(own, s, -jnp.inf)
        e = jnp.exp(s - jnp.max(s, axis=-1, keepdims=True))
        p = e / jnp.sum(e, axis=-1, keepdims=True)
        o = jnp.dot(p.astype(BF16), v, preferred_element_type=F32)
        o_ref[:, sl] = (o * _silu(g_ref[:, sl])).astype(o_ref.dtype)


def _mem_attend(z_all, q_col0, g_col0, mem_k, mem_v, layer, seq_len, nh, *, tq, joint, out_dtype):
    m = z_all.shape[0]
    mw = mem_k.shape[2]
    n = m // seq_len
    mem_len = mem_k.shape[1] // n
    assert q_col0 % mw == 0 and g_col0 % mw == 0
    if joint:
        tq, kv_rows, rps = m, n * mem_len, seq_len
        kv_idx = lambda i: (layer, 0, 0)
    else:
        tq = min(tq, seq_len)
        assert seq_len % tq == 0
        tps = seq_len // tq
        kv_rows, rps = mem_len, None
        kv_idx = lambda i: (layer, i // tps, 0)
    kv = pl.BlockSpec((None, kv_rows, mw), kv_idx)
    return pl.pallas_call(
        partial(_mem_body, nh=nh, hd=mw // nh, rows_per_seq=rps, keys_per_seq=mem_len),
        grid=(m // tq,),
        in_specs=[pl.BlockSpec((tq, mw), lambda i: (i, q_col0 // mw)),
                  pl.BlockSpec((tq, mw), lambda i: (i, g_col0 // mw)), kv, kv],
        out_specs=pl.BlockSpec((tq, mw), lambda i: (i, 0)),
        out_shape=jax.ShapeDtypeStruct((m, mw), out_dtype),
        compiler_params=_params("parallel"),
        name="mem_attend",
    )(z_all, z_all, mem_k, mem_v)


def _log_sigmoid(x):
    return jnp.minimum(x, 0.0) - jnp.log1p(jnp.exp(-jnp.abs(x)))


def _split3(x):
    hi = x.astype(BF16)
    r = x - hi.astype(F32)
    mid = r.astype(BF16)
    return hi, mid, (r - mid.astype(F32)).astype(BF16)


def _lf_body(f_ref, bf_ref, lf_ref, c_ref, ck3_ref, r_ref, *, tk):
    lf = _log_sigmoid(f_ref[...] + bf_ref[...])
    lf_ref[...] = lf
    t = lf.shape[0]
    row = lax.broadcasted_iota(jnp.int32, lf.shape, 0)
    c, s = lf, 1
    while s < t:
        c = c + jnp.where(row >= s, pltpu.roll(c, s, 0), 0.0)
        s *= 2
    c_ref[...] = c
    for j in range(t // tk):
        rows = slice(j * tk, (j + 1) * tk)
        r = c[(j + 1) * tk - 1:(j + 1) * tk, :]
        r_ref[j:j + 1, :] = r
        for part, term in enumerate(_split3((r - c[rows, :]) * LOG2E)):
            ck3_ref[rows, part * LANES:(part + 1) * LANES] = term


def _forget_gates(z_tail, f_col0, b_f128, layer, n, seq_len, *, tk):
    assert f_col0 % LANES == 0 and seq_len % tk == 0
    z3 = z_tail.reshape(n, seq_len, z_tail.shape[1])
    blk = pl.BlockSpec((None, seq_len, LANES), lambda b: (b, 0, 0))
    return pl.pallas_call(
        partial(_lf_body, tk=tk),
        grid=(n,),
        in_specs=[pl.BlockSpec((None, seq_len, LANES), lambda b: (b, 0, f_col0 // LANES)),
                  pl.BlockSpec((None, 1, LANES), lambda b: (layer, 0, 0))],
        out_specs=[blk, blk, pl.BlockSpec((None, seq_len, 3 * LANES), lambda b: (b, 0, 0)),
                   pl.BlockSpec((None, seq_len // tk, LANES), lambda b: (b, 0, 0))],
        out_shape=[jax.ShapeDtypeStruct((n, seq_len, LANES), F32), jax.ShapeDtypeStruct((n, seq_len, LANES), F32),
                   jax.ShapeDtypeStruct((n, seq_len, 3 * LANES), BF16),
                   jax.ShapeDtypeStruct((n, seq_len // tk, LANES), F32)],
        compiler_params=_params("parallel"),
        name="forget_gates",
    )(z3, b_f128)


def _lf_step_body(f_ref, bf_ref, lf_ref, c_ref, *, steps):
    lf = _log_sigmoid(f_ref[...] + bf_ref[...])
    lf_ref[...] = lf
    step = lax.broadcasted_iota(jnp.int32, lf.shape, 0) % steps
    c, s = lf, 1
    while s < steps:
        c = c + jnp.where(step >= s, pltpu.roll(c, s, 0), 0.0)
        s *= 2
    c_ref[...] = c


def _forget_gates_step(z_tail, f_col0, b_f128, layer, steps):
    m = z_tail.shape[0]
    out = pl.BlockSpec((m, LANES), lambda i: (0, 0))
    return pl.pallas_call(
        partial(_lf_step_body, steps=steps),
        grid=(1,),
        in_specs=[pl.BlockSpec((m, LANES), lambda i: (0, f_col0 // LANES)),
                  pl.BlockSpec((None, 1, LANES), lambda i: (layer, 0, 0))],
        out_specs=[out, out],
        out_shape=[jax.ShapeDtypeStruct((m, LANES), F32), jax.ShapeDtypeStruct((m, LANES), F32)],
        compiler_params=_params("arbitrary"),
        name="forget_gates_step",
    )(z_tail, b_f128)


def _fox_body(q_ref, k_ref, v_ref, g_ref, c_ref, ck3_ref, r_ref, o_ref, ka, va, m_s, acc, *, tq, tk, hd, hps):
    h0 = pl.program_id(1) * hps
    qi = pl.program_id(2)
    seq_len = k_ref.shape[0]
    cols = lambda u: slice(u * hd, (u + 1) * hd)

    @pl.when(qi == 0)
    def _():
        sel_r = lax.broadcasted_iota(jnp.int32, (3 * LANES, hd), 0)
        sel_c = lax.broadcasted_iota(jnp.int32, (3 * LANES, hd), 1)
        one0 = jnp.where(lax.broadcasted_iota(jnp.int32, (tq, hd), 1) == 0, 1.0, 0.0).astype(BF16)
        for u in range(hps):
            sel = ((sel_r % LANES == h0 + u) & (sel_r // LANES == sel_c)).astype(BF16)
            for r0 in range(0, seq_len, tq):
                rows = slice(r0, r0 + tq)
                ka[u, rows, :hd] = k_ref[rows, cols(u)].astype(BF16)
                ka[u, rows, hd:] = jnp.dot(ck3_ref[rows, :], sel, preferred_element_type=F32).astype(BF16)
                va[u, rows, :hd] = v_ref[rows, cols(u)].astype(BF16)
                va[u, rows, hd:] = one0

    lane = lax.broadcasted_iota(jnp.int32, c_ref.shape, 1)
    lane1 = lax.broadcasted_iota(jnp.int32, (1, LANES), 1)
    ones3 = jnp.where(lax.broadcasted_iota(jnp.int32, (tq, hd), 1) < 3, 1.0, 0.0).astype(BF16)
    cq, qa = [], []
    for u in range(hps):
        cq.append(jnp.broadcast_to(jnp.sum(jnp.where(lane == h0 + u, c_ref[...], 0.0), axis=1, keepdims=True),
                                   (tq, LANES)))
        qa.append(jnp.concatenate([(q_ref[:, cols(u)] * (hd ** -0.5 * LOG2E)).astype(BF16), ones3], axis=1))
    m_s[...] = jnp.full(m_s.shape, -jnp.inf, F32)
    acc[...] = jnp.zeros(acc.shape, F32)
    reps = tk // LANES

    def kv_step(j, masked):
        ks = pl.ds(pl.multiple_of(j * tk, tk), tk)
        rj_all = r_ref[pl.ds(j, 1), :]
        for u in range(hps):
            s = lax.dot_general(qa[u], ka[u, ks, :], NT_DIMS, preferred_element_type=F32)
            if masked:
                key_minus_query = (lax.broadcasted_iota(jnp.int32, (tq, tk), 1)
                                   - lax.broadcasted_iota(jnp.int32, (tq, tk), 0))
                s = jnp.where(key_minus_query <= qi * tq - j * tk, s, -jnp.inf)
            rj = jnp.sum(jnp.where(lane1 == h0 + u, rj_all, 0.0), axis=1, keepdims=True)
            shift = (cq[u] - rj) * LOG2E
            m_old = m_s[u]
            m_new = jnp.maximum(m_old, jnp.broadcast_to(jnp.max(s, axis=1, keepdims=True), (tq, LANES)) + shift)
            p = jnp.exp2(s - jnp.concatenate([m_new - shift] * reps, axis=1))
            a = jnp.exp2(m_old - m_new)
            acc[u] = (jnp.concatenate([a, a], axis=1) * acc[u]
                      + jnp.dot(p.astype(BF16), va[u, ks, :], preferred_element_type=F32))
            m_s[u] = m_new

    n_full = (qi * tq) // tk
    lax.fori_loop(0, n_full, lambda j, c: (kv_step(j, False), c)[1], 0)
    for d in range(tq // tk):
        kv_step(n_full + d, True)
    for u in range(hps):
        o = acc[u]
        o_ref[:, cols(u)] = (o[:, :hd] / jnp.broadcast_to(o[:, hd:hd + 1], (tq, hd))
                             * _silu(g_ref[:, cols(u)])).astype(o_ref.dtype)


def _fox_attend(zq, zk, zv, zg, c, ck3, r, n, seq_len, nh, hd, *, tq, tk, hps):
    assert seq_len % tq == 0 and tq % tk == 0 and r.shape[1] == seq_len // tk and hd == LANES and nh % hps == 0
    r3 = lambda a: a.reshape(n, seq_len, nh * hd)
    qblk = pl.BlockSpec((None, tq, hps * hd), lambda b, h, i: (b, i, h))
    kvblk = pl.BlockSpec((None, seq_len, hps * hd), lambda b, h, i: (b, 0, h))
    return pl.pallas_call(
        partial(_fox_body, tq=tq, tk=tk, hd=hd, hps=hps),
        grid=(n, nh // hps, seq_len // tq),
        in_specs=[qblk, kvblk, kvblk, qblk,
                  pl.BlockSpec((None, tq, LANES), lambda b, h, i: (b, i, 0)),
                  pl.BlockSpec((None, seq_len, 3 * LANES), lambda b, h, i: (b, 0, 0)),
                  pl.BlockSpec((None, seq_len // tk, LANES), lambda b, h, i: (b, 0, 0))],
        out_specs=qblk,
        out_shape=jax.ShapeDtypeStruct((n, seq_len, nh * hd), BF16),
        scratch_shapes=[pltpu.VMEM((hps, seq_len, 2 * hd), BF16), pltpu.VMEM((hps, seq_len, 2 * hd), BF16),
                        pltpu.VMEM((hps, tq, LANES), F32), pltpu.VMEM((hps, tq, 2 * hd), F32)],
        compiler_params=_params("parallel", "parallel", "arbitrary"),
        name="fox_attend",
    )(r3(zq), r3(zk), r3(zv), r3(zg), c, ck3, r).reshape(n * seq_len, nh * hd)


def _page_sums_body(lf_ref, after_ref, tot_ref, *, ps):
    lf = lf_ref[...]
    n = lf.shape[1]
    key = lax.broadcasted_iota(jnp.int32, lf.shape, 1) % ps
    sfx, s = lf, 1
    while s < ps:
        sfx = sfx + jnp.where(key + s < ps, pltpu.roll(sfx, n - s, 1), 0.0)
        s *= 2
    tot, s = jnp.where(key == 0, sfx, 0.0), 1
    while s < ps:
        tot = tot + pltpu.roll(tot, s, 1)
        s *= 2
    src = lax.broadcasted_iota(jnp.int32, (n, n), 0)
    dst = lax.broadcasted_iota(jnp.int32, (n, n), 1)
    perm = (dst == (src % ps) * (n // ps) + src // ps).astype(BF16)
    for x, o_ref in ((sfx - lf, after_ref), (tot, tot_ref)):
        o_ref[...] = sum(jnp.dot(term, perm, preferred_element_type=F32) for term in _split3(x))


def _page_sums(lf_hk, ps, *, rows):
    nl, n_rows, n = lf_hk.shape
    rows = min(rows, n_rows)
    assert n_rows % rows == 0 and n % ps == 0
    blk = pl.BlockSpec((None, rows, n), lambda l, i: (l, i, 0))
    out = jax.ShapeDtypeStruct(lf_hk.shape, F32)
    return pl.pallas_call(
        partial(_page_sums_body, ps=ps),
        grid=(nl, n_rows // rows),
        in_specs=[blk], out_specs=[blk, blk], out_shape=[out, out],
        compiler_params=_params("parallel", "parallel"),
        name="page_forget_sums",
    )(lf_hk)


def _fox_step_body(pt_ref, q_ref, kn_ref, vn_ref, cn_ref, *rest, steps, hd, n_steps, pps):
    kc_refs, vc_refs, after_refs, tot_refs = (rest[k * pps:(k + 1) * pps] for k in range(4))
    o_ref, qb, own_s, m_s, l_s, acc, run = rest[4 * pps:]
    p = pl.program_id(1)
    n_groups = q_ref.shape[0]
    rows = steps * SUBLANES
    keys = kc_refs[0].shape[0] * SUBLANES

    def same_head(n_keys):
        r = lax.broadcasted_iota(jnp.int32, (rows, n_keys), 0) % SUBLANES
        c = lax.broadcasted_iota(jnp.int32, (rows, n_keys), 1) % SUBLANES
        return jnp.where(r == c, 0.0, -jnp.inf)

    def attend(k_b, v_b, bias):
        groups = range(n_groups)
        s = [lax.dot_general(qb[g], k_b[g], NT_DIMS, preferred_element_type=F32) + bias[g] for g in groups]
        m_old = [m_s[g] for g in groups]
        m_new = [jnp.maximum(m_old[g], jnp.max(s[g], axis=1, keepdims=True)) for g in groups]
        e = [jnp.exp2(s[g] - m_new[g]) for g in groups]
        a = [jnp.exp2(m_old[g] - m_new[g]) for g in groups]
        pv = [jnp.dot(e[g].astype(BF16), v_b[g], preferred_element_type=F32) for g in groups]
        for g in groups:
            l_s[g] = a[g] * l_s[g] + jnp.sum(e[g], axis=1, keepdims=True)
            acc[g] = a[g] * acc[g] + pv[g]
            m_s[g] = m_new[g]

    @pl.when(p == 0)
    def _():
        m_s[...] = jnp.full(m_s.shape, -jnp.inf, F32)
        l_s[...] = jnp.zeros(l_s.shape, F32)
        acc[...] = jnp.zeros(acc.shape, F32)
        run[...] = jnp.zeros(run.shape, F32)
        own_s[...] = same_head(keys)
        key_step = lax.broadcasted_iota(jnp.int32, (rows, rows), 1) // SUBLANES
        q_step = lax.broadcasted_iota(jnp.int32, (rows, rows), 0) // SUBLANES
        visible = jnp.where(key_step <= q_step, same_head(rows), -jnp.inf)
        for g in range(n_groups):
            qb[g] = (q_ref[g] * (hd ** -0.5 * LOG2E)).astype(BF16)
        attend([kn_ref[g].astype(BF16) for g in range(n_groups)],
               [vn_ref[g].astype(BF16) for g in range(n_groups)],
               [visible - LOG2E * cn_ref[g] for g in range(n_groups)])

    running, bias = run[...], []
    for k in range(pps):
        bias.append(LOG2E * (after_refs[k][...] + running))
        running = running + tot_refs[k][...]
    run[...] = running
    page_rows = lambda refs, g: jnp.concatenate([r[:, g].reshape(keys, hd).astype(BF16) for r in refs], axis=0)
    attend([page_rows(kc_refs, g) for g in range(n_groups)],
           [page_rows(vc_refs, g) for g in range(n_groups)],
           [jnp.concatenate([own_s[...] + b[g:g + 1, :] for b in bias], axis=1) for g in range(n_groups)])

    @pl.when(p == n_steps - 1)
    def _():
        o_ref[...] = acc[...] / l_s[...]


def _fox_attend_step(q, k_new, v_new, c_new, kc, vc, after, tot, page_table, layer, steps, *, pps):
    n, n_groups, rows, hd = q.shape
    ps = kc.shape[2]
    keys = ps * SUBLANES
    n_pages = page_table.shape[1]
    assert rows == steps * SUBLANES and kc.shape[3:] == (n_groups, SUBLANES, hd) and keys % LANES == 0
    assert n_pages % pps == 0
    per_seq = lambda shape: pl.BlockSpec((None,) + shape, lambda b, p, pt: (b,) + (0,) * len(shape))

    def pages(shape):
        return [pl.BlockSpec((None, None) + shape,
                             lambda b, p, pt, k=k: (layer, pt[(b + 1) * n_pages - 1 - p * pps - k]) + (0,) * len(shape))
                for k in range(pps)]

    grid_spec = pltpu.PrefetchScalarGridSpec(
        num_scalar_prefetch=1,
        grid=(n, n_pages // pps),
        in_specs=[per_seq((n_groups, rows, hd)), per_seq((n_groups, rows, hd)), per_seq((n_groups, rows, hd)),
                  per_seq((n_groups, 1, rows)),
                  *pages((ps, n_groups, SUBLANES, hd)), *pages((ps, n_groups, SUBLANES, hd)),
                  *pages((n_groups, keys)), *pages((n_groups, keys))],
        out_specs=per_seq((n_groups, rows, hd)),
        scratch_shapes=[pltpu.VMEM((n_groups, rows, hd), BF16), pltpu.VMEM((rows, keys), F32),
                        pltpu.VMEM((n_groups, rows, 1), F32), pltpu.VMEM((n_groups, rows, 1), F32),
                        pltpu.VMEM((n_groups, rows, hd), F32), pltpu.VMEM((n_groups, keys), F32)],
    )
    return pl.pallas_call(
        partial(_fox_step_body, steps=steps, hd=hd, n_steps=n_pages // pps, pps=pps),
        grid_spec=grid_spec,
        out_shape=jax.ShapeDtypeStruct((n, n_groups, rows, hd), F32),
        compiler_params=_params("arbitrary", "arbitrary"),
        name="fox_attend_step",
    )(page_table.reshape(-1), q, k_new, v_new, c_new, *([kc] * pps), *([vc] * pps), *([after] * pps), *([tot] * pps))


def _offsets(sizes):
    out, o = [], 0
    for s in sizes:
        out.append(o)
        o += s
    return out, o


def _ssm_conv_mix(z, seq_len, state_re, state_im, conv_state, mem_k, mem_v, s5p, glu_w, glu_b, conv_w, conv_b,
                  i, layer, nh_mem, cols, *, decode):
    m = z.shape[0]
    n = m // seq_len
    w, c = glu_w.shape[1], conv_w.shape[2]
    o_ua, o_ga, o_bg, o_cg, o_hb, o_gb, o_qm, o_gm = cols
    sps = seq_len if decode else seq_len // SUBLANES
    if decode:
        to_scan = lambda a: a.reshape(n, seq_len, -1).transpose(1, 0, 2)
        u = to_scan(z[:, o_ua:o_ua + w]).reshape(1, m, w)
        h0_re, h0_im = state_re.reshape(1, n, -1), state_im.reshape(1, n, -1)
    else:
        u = z[:, o_ua:o_ua + w].reshape(n, SUBLANES, sps, w).transpose(0, 2, 1, 3).reshape(n, seq_len, w)
        pad = lambda s: jnp.pad(s.reshape(n, 1, -1), ((0, 0), (0, SUBLANES - 1), (0, 0)))
        h0_re, h0_im = pad(state_re), pad(state_im)
    y, h_re, h_im = _s5(u, h0_re, h0_im, s5p, chain=not decode)
    if decode:
        y = y.reshape(seq_len, n, w).transpose(1, 0, 2).reshape(m, w)
        h_re, h_im = h_re[0], h_im[0]
    else:
        y = y.reshape(n, sps, SUBLANES, w).transpose(0, 2, 1, 3).reshape(m, w)
        h_re, h_im = h_re[:, SUBLANES - 1], h_im[:, SUBLANES - 1]
    mix_dtype = F32 if decode else BF16
    mix_a = _glu_gate(y, z, o_ga, glu_w, glu_b, i, tm=1024, tn=512, out_dtype=mix_dtype)
    if decode:
        bg, cg, hb, gb = (to_scan(z[:, o:o + c]) for o in (o_bg, o_cg, o_hb, o_gb))
        mix_b, conv_new = _conv_gate_step(bg, cg, hb, gb, conv_state.transpose(1, 0, 2), conv_w, conv_b, i, ct=512)
        mix_b = mix_b.transpose(1, 0, 2).reshape(m, c)
        conv_new = conv_new.transpose(1, 0, 2)
    else:
        mix_b, conv_new = _conv_gate(z, (o_bg, o_cg, o_hb, o_gb), c, conv_state, conv_w, conv_b, i, seq_len,
                                     tm=512, ct=512)
    mix_m = _mem_attend(z, o_qm, o_gm, mem_k, mem_v, layer, seq_len, nh_mem, tq=512, joint=decode,
                        out_dtype=mix_dtype)
    gp = state_re.shape[1:]
    return (mix_a, mix_b, mix_m), (h_re.reshape((n,) + gp), h_im.reshape((n,) + gp), conv_new)


def _ssm_conv_layer(xp, xs, t_p, t_s, states_s, mem_p, mem_s, w_in, w_out, s5p, glu_w, glu_b, conv_w, conv_b,
                    ln_g, ln_b, i, layer, alpha, nh_mem):
    d = xp[0].shape[1]
    n_p = xp[0].shape[0] // t_p
    w, c, mw = glu_w.shape[1], conv_w.shape[2], mem_p[0].shape[2]
    cols, n_cols = _offsets((w, w, c, c, c, c, mw, mw))
    z_p, z_s = _matmul([(xp[1], xs[1], 0, w_in, i, 0, 0, d)], n_cols, tm=1024, tn=512, name="in_proj_ssm")
    zero_state = jnp.zeros((n_p,) + states_s[0].shape[1:], F32)
    zero_conv = jnp.zeros((n_p,) + states_s[2].shape[1:], F32)
    branch = (s5p, glu_w, glu_b, conv_w, conv_b, i, layer, nh_mem, cols)
    mix_p, st_p = _ssm_conv_mix(z_p, t_p, zero_state, zero_state, zero_conv, *mem_p, *branch, decode=False)
    mix_s, st_s = _ssm_conv_mix(z_s, t_s, *states_s, *mem_s, *branch, decode=True)
    out_proj = lambda mix, x: _matmul([(mix[0], None, 0, w_out, i, 0, 0, w), (mix[1], None, 0, w_out, i, w // c, 0, c),
                                       (mix[2], None, 0, w_out, i, (w + c) // mw, 0, mw)],
                                      d, tm=1024, tn=256, res=x[0], alpha=alpha, name="out_proj_ssm")
    y_p, y_s = out_proj(mix_p, xp), out_proj(mix_s, xs)
    return _layer_norm(y_p, ln_g, ln_b, layer, tm=256), _layer_norm(y_s, ln_g, ln_b, layer, tm=256), st_p, st_s


def _fox_layer(xp, xs, t_p, t_s, past, mem_p, mem_s, w_in_t, w_tail_t, b_f128, w_out, ln_g, ln_b, i, layer, alpha,
               nh, nh_mem):
    d = xp[0].shape[1]
    m_p, m_s = xp[0].shape[0], xs[0].shape[0]
    n_p, n_s = m_p // t_p, m_s // t_s
    mw = mem_p[0].shape[2]
    fw = w_out.shape[1] - mw
    hd = fw // nh
    proj = [_matmul([(xp[1], xs[1], 0, w_in_t, i, 0, o * fw, d)], fw, tm=1024, tn=512, transposed=True,
                    name="in_proj_fox") for o in range(4)]
    (zq_p, zq_s), (zk_p, zk_s), (zv_p, zv_s), (zg_p, zg_s) = proj
    tail_p, tail_s = _matmul([(xp[1], xs[1], 0, w_tail_t, i, 0, 0, d)], w_tail_t.shape[1], tm=1024, tn=384,
                             transposed=True, name="in_proj_fox_tail")
    o_qm, o_gm, o_f = 0, mw, 2 * mw

    tq = tk = min(512, t_p)
    lf_p, c, ck3, r = _forget_gates(tail_p, o_f, b_f128, i, n_p, t_p, tk=tk)
    mix_f_p = _fox_attend(zq_p, zk_p, zv_p, zg_p, c, ck3, r, n_p, t_p, nh, hd, tq=tq, tk=tk, hps=4)
    mix_m_p = _mem_attend(tail_p, o_qm, o_gm, *mem_p, layer, t_p, nh_mem, tq=512, joint=False, out_dtype=BF16)

    kc, vc, after, tot, page_table = past
    ng = nh // SUBLANES
    lf_s, c_new = _forget_gates_step(tail_s, o_f, b_f128, i, t_s)
    grouped = lambda a: (a.reshape(n_s, t_s, ng, SUBLANES, -1).transpose(0, 2, 1, 3, 4)
                         .reshape(n_s, ng, t_s * SUBLANES, -1))
    c_new = grouped(c_new[:, :nh]).reshape(n_s, ng, 1, t_s * SUBLANES)
    y = _fox_attend_step(grouped(zq_s), grouped(zk_s), grouped(zv_s), c_new, kc, vc, after, tot, page_table, i, t_s,
                         pps=2)
    y = y.reshape(n_s, ng, t_s, SUBLANES, hd).transpose(0, 2, 1, 3, 4).reshape(m_s, fw)
    mix_f_s = _gate_step(y, zg_s)
    mix_m_s = _mem_attend(tail_s, o_qm, o_gm, *mem_s, layer, t_s, nh_mem, tq=512, joint=True, out_dtype=F32)

    out_proj = lambda mix_f, mix_m, x: _matmul([(mix_f, None, 0, w_out, i, 0, 0, fw),
                                                (mix_m, None, 0, w_out, i, fw // mw, 0, mw)],
                                               d, tm=1024, tn=256, res=x[0], alpha=alpha, name="out_proj_fox")
    y_p, y_s = out_proj(mix_f_p, mix_m_p, xp), out_proj(mix_f_s, mix_m_s, xs)
    kv_p = (zk_p.reshape(n_p, t_p, nh, hd), zv_p.reshape(n_p, t_p, nh, hd), lf_p[:, :, :nh])
    kv_s = (zk_s.reshape(n_s, t_s, nh, hd), zv_s.reshape(n_s, t_s, nh, hd), lf_s[:, :nh].reshape(n_s, t_s, nh))
    return _layer_norm(y_p, ln_g, ln_b, layer, tm=256), _layer_norm(y_s, ln_g, ln_b, layer, tm=256), kv_p, kv_s


def _gate_step_body(y_ref, g_ref, o_ref):
    o_ref[...] = y_ref[...] * _silu(g_ref[...])


def _gate_step(y, g):
    m, w = y.shape
    blk = pl.BlockSpec((m, w), lambda i: (0, 0))
    return pl.pallas_call(_gate_step_body, grid=(1,), in_specs=[blk, blk], out_specs=blk,
                          out_shape=jax.ShapeDtypeStruct((m, w), F32), compiler_params=_params("arbitrary"),
                          name="gate_step")(y, g)


def kernel(x_prompt, x_sample, cache_fox_k, cache_fox_v, cache_fox_lf, cache_mem_k, cache_mem_v, state_ssm_re, state_ssm_im, state_conv, page_table, mem_prompt, w_in_ssm, w_out_ssm, ssm_lam_re, ssm_lam_im, ssm_log_dt, ssm_b_re, ssm_b_im, ssm_c_re, ssm_c_im, ssm_d, glu_w, glu_b, conv_w, conv_b, w_in_fox, b_fgate, w_out_fox, w_mem_k, w_mem_v, ln_g, ln_b):
    n_p, t_p, d = x_prompt.shape
    n_s, t_s, _ = x_sample.shape
    depth = ln_g.shape[0]
    alpha = (2 * depth) ** 0.25
    nh_fox = b_fgate.shape[1]
    nh_mem = cache_mem_k.shape[3]
    mw = w_mem_k.shape[2]
    mem_len = mem_prompt.shape[1]
    fw = w_out_fox.shape[1] - mw
    n_fox, n_pool, ps = cache_fox_k.shape[:3]

    w_in_t = w_in_fox.transpose(0, 2, 1)
    w_tail_t = jnp.concatenate([w_in_t[:, 4 * fw + nh_fox:], w_in_t[:, 4 * fw:4 * fw + nh_fox]], axis=1)
    w_tail_t = jnp.pad(w_tail_t, ((0, 0), (0, LANES - nh_fox), (0, 0)))
    b_f128 = jnp.pad(b_fgate.astype(F32), ((0, 0), (0, LANES - nh_fox))).reshape(n_fox, 1, LANES)
    assert nh_fox % SUBLANES == 0
    ng = nh_fox // SUBLANES
    kc = cache_fox_k.reshape(n_fox, n_pool, ps, ng, SUBLANES, fw // nh_fox)
    vc = cache_fox_v.reshape(n_fox, n_pool, ps, ng, SUBLANES, fw // nh_fox)
    lf_hk = (cache_fox_lf.astype(F32).transpose(0, 1, 3, 2)
             .reshape(n_fox, n_pool * ng, SUBLANES * ps))
    lf_after, lf_tot = (a.reshape(n_fox, n_pool, ng, ps * SUBLANES) for a in _page_sums(lf_hk, ps, rows=512))
    past = (kc, vc, lf_after, lf_tot, page_table)
    mem_s = (cache_mem_k.reshape(depth, n_s * mem_len, mw), cache_mem_v.reshape(depth, n_s * mem_len, mw))

    mem_in = mem_prompt.reshape(n_p * mem_len, d).astype(BF16)
    mem_p = (_memproj(mem_in, w_mem_k), _memproj(mem_in, w_mem_v))

    xp = (x_prompt.reshape(n_p * t_p, d),)
    xs = (x_sample.reshape(n_s * t_s, d),)
    xp, xs = xp + (xp[0].astype(BF16),), xs + (xs[0].astype(BF16),)

    fox_p, fox_s, ssm_p, ssm_s = [], [], [], []
    for layer in range(depth):
        i = layer // 2
        if layer % 2 == 0:
            s5p = _s5_discretise(ssm_lam_re[i], ssm_lam_im[i], ssm_log_dt[i], ssm_b_re[i], ssm_b_im[i],
                                 ssm_c_re[i], ssm_c_im[i], ssm_d[i])
            states_s = (state_ssm_re[i].astype(F32), state_ssm_im[i].astype(F32), state_conv[i].astype(F32))
            xp, xs, st_p, st_s = _ssm_conv_layer(xp, xs, t_p, t_s, states_s, mem_p, mem_s, w_in_ssm, w_out_ssm, s5p,
                                                 glu_w, glu_b, conv_w, conv_b, ln_g, ln_b, i, layer, alpha, nh_mem)
            ssm_p.append(st_p), ssm_s.append(st_s)
        else:
            xp, xs, kv_p, kv_s = _fox_layer(xp, xs, t_p, t_s, past, mem_p, mem_s, w_in_t, w_tail_t, b_f128,
                                            w_out_fox, ln_g, ln_b, i, layer, alpha, nh_fox, nh_mem)
            fox_p.append(kv_p), fox_s.append(kv_s)
    stack = lambda rows, k: jnp.stack([r[k] for r in rows])
    mem_shape = (depth, n_p, mem_len, nh_mem, mw // nh_mem)
    return (xp[0].reshape(n_p, t_p, d), xs[0].reshape(n_s, t_s, d),
            stack(fox_p, 0), stack(fox_p, 1), stack(fox_p, 2),
            stack(fox_s, 0), stack(fox_s, 1), stack(fox_s, 2),
            stack(ssm_p, 0), stack(ssm_p, 1), stack(ssm_p, 2),
            stack(ssm_s, 0), stack(ssm_s, 1), stack(ssm_s, 2),
            mem_p[0].reshape(mem_shape), mem_p[1].reshape(mem_shape))
```
